```python
import math
import jax, jax.numpy as jnp
from jax import lax
import numpy as np

D_MODEL = 1024
BATCH = 2
SEQ = 8192
DEPTH = 2

CHUNK = 64
Q_BLOCK = 128
HEAD_DIM = 64
EPS = 1e-6
ROPE_THETA = 500000.0
ROPE_DIMS = HEAD_DIM // 4

A_HEADS = 4
A_VDIM = 2 * HEAD_DIM
B_HEADS = 8
B_LEFT_CHUNKS = 8
B_BAND = (B_LEFT_CHUNKS + 1) * CHUNK
REL_CLIP = 128
C_HEADS = 8

BRANCH_WIDTH = 512
N_BRANCH = 3

PEER_HEADS = 8
N_KEYS = 128
N_EXPERTS = N_KEYS * N_KEYS
PEER_QDIM = 256
PEER_HALF = PEER_QDIM // 2
PEER_TOPK = 16
PEER_TOKEN_BLOCK = 128

A_QK_W = A_HEADS * 2 * HEAD_DIM
A_V_W = A_HEADS * A_VDIM
B_W = B_HEADS * HEAD_DIM
C_W = C_HEADS * HEAD_DIM
GATE_W = N_BRANCH * D_MODEL
SPLIT_SIZES = (A_QK_W, A_QK_W, A_V_W, B_W, B_W, B_W, C_W, C_W, C_W, GATE_W)
SPLIT_POINTS = tuple(int(v) for v in np.cumsum(SPLIT_SIZES)[:-1])
IN_COLS = sum(SPLIT_SIZES)

kernel_name = 'hybrid_diff_band_stickbreak_peer'


def rms_norm(x, g):
    xf = x.astype(jnp.float32)
    y = xf * lax.rsqrt(jnp.mean(xf * xf, axis=-1, keepdims=True) + EPS)
    return (y * g.astype(jnp.float32)).astype(x.dtype)


def rope_tables(seq):
    inv = 1.0 / (ROPE_THETA ** (jnp.arange(0, ROPE_DIMS, 2, dtype=jnp.float32) / ROPE_DIMS))
    ang = jnp.arange(seq, dtype=jnp.float32)[:, None] * inv[None, :]
    return jnp.cos(ang), jnp.sin(ang)


def apply_partial_rope(x, cos, sin):
    half = ROPE_DIMS // 2
    x1, x2, xp = x[..., :half], x[..., half:ROPE_DIMS], x[..., ROPE_DIMS:]
    c = cos.astype(x.dtype)
    s = sin.astype(x.dtype)
    return jnp.concatenate([x1 * c - x2 * s, x2 * c + x1 * s, xp], axis=-1)


def diff_attention(q, k, v, lam, subln_g, lam_init):
    B, H, _, S, d = q.shape
    nb = S // Q_BLOCK
    qb = jnp.moveaxis(q.reshape(B, H, 2, nb, Q_BLOCK, d), 3, 0)
    key_chunk = jnp.arange(S) // CHUNK
    scale = d ** -0.5

    def one_block(args):
        qblk, i = args
        s = jnp.einsum('bhcqd,bhckd->bhcqk', qblk, k).astype(jnp.float32) * scale
        q_chunk = (i * Q_BLOCK + jnp.arange(Q_BLOCK)) // CHUNK
        mask = key_chunk[None, :] <= q_chunk[:, None]
        s = jnp.where(mask, s, -jnp.inf)
        p = jax.nn.softmax(s, axis=-1)
        w = p[:, :, 0] - lam * p[:, :, 1]
        return jnp.einsum('bhqk,bhkv->bhqv', w.astype(v.dtype), v)

    o = lax.map(one_block, (qb, jnp.arange(nb)))
    o = jnp.moveaxis(o, 0, 2).reshape(B, H, S, v.shape[-1])
    return rms_norm(o, subln_g) * (1.0 - lam_init)


def chunk_band_attention(q, k, v, rel_bias):
    B, H, S, d = q.shape
    nc = S // CHUNK
    qc = q.reshape(B, H, nc, CHUNK, d)

    def band(t):
        tc = t.reshape(B, H, nc, CHUNK, d)
        tp = jnp.pad(tc, ((0, 0), (0, 0), (B_LEFT_CHUNKS, 0), (0, 0), (0, 0)))
        return jnp.concatenate([tp[:, :, j:j + nc] for j in range(B_LEFT_CHUNKS + 1)], axis=3)

    kb, vb = band(k), band(v)
    s = jnp.einsum('bhcqd,bhckd->bhcqk', qc, kb).astype(jnp.float32) * (d ** -0.5)
    qi = jnp.arange(CHUNK)[:, None]
    m = jnp.arange(B_BAND)[None, :]
    dist = qi + B_LEFT_CHUNKS * CHUNK - m
    rel_idx = jnp.clip(dist, -REL_CLIP, REL_CLIP) + REL_CLIP
    bias = rel_bias[:, rel_idx].astype(jnp.float32)
    valid = (jnp.arange(nc)[:, None] + (jnp.arange(B_BAND) // CHUNK)[None, :] - B_LEFT_CHUNKS) >= 0
    s = jnp.where(valid[None, None, :, None, :], s + bias[None, :, None], -jnp.inf)
    p = jax.nn.softmax(s, axis=-1)
    o = jnp.einsum('bhcqk,bhckd->bhcqd', p.astype(vb.dtype), vb)
    return o.reshape(B, H, S, d)


def stick_breaking_attention(q, k, v):
    B, H, S, d = q.shape
    nb = S // Q_BLOCK
    qb = jnp.moveaxis(q.reshape(B, H, nb, Q_BLOCK, d), 2, 0)
    key_pos = jnp.arange(S)
    scale = d ** -0.5

    def one_block(args):
        qblk, i = args
        z = jnp.einsum('bhqd,bhkd->bhqk', qblk, k).astype(jnp.float32) * scale
        q_pos = i * Q_BLOCK + jnp.arange(Q_BLOCK)
        strict = key_pos[None, :] < q_pos[:, None]
        log_keep = jnp.where(strict, jax.nn.log_sigmoid(-z), 0.0)
        later = lax.cumsum(log_keep, axis=3, reverse=True) - log_keep
        a = jnp.where(strict, jnp.exp(jax.nn.log_sigmoid(z) + later), 0.0)
        return jnp.einsum('bhqk,bhkd->bhqd', a.astype(v.dtype), v)

    o = lax.map(one_block, (qb, jnp.arange(nb)))
    return jnp.moveaxis(o, 0, 2).reshape(B, H, S, d)


def peer(xn, wq, subkeys, u, v):
    B, S, D = xn.shape
    T = B * S
    xt = xn.reshape(T, D)
    q = (xt @ wq).reshape(T, PEER_HEADS, 2, PEER_HALF)
    s = jnp.einsum('thpc,phnc->thpn', q, subkeys).astype(jnp.float32)
    top_s, top_i = lax.top_k(s, PEER_TOPK)
    cand_s = top_s[:, :, 0, :, None] + top_s[:, :, 1, None, :]
    cand_i = top_i[:, :, 0, :, None] * N_KEYS + top_i[:, :, 1, None, :]
    best_s, best_pos = lax.top_k(cand_s.reshape(T, PEER_HEADS, PEER_TOPK * PEER_TOPK), PEER_TOPK)
    expert = jnp.take_along_axis(cand_i.reshape(T, PEER_HEADS, PEER_TOPK * PEER_TOPK), best_pos, axis=-1)
    g = jax.nn.softmax(best_s, axis=-1)
    nb = T // PEER_TOKEN_BLOCK

    def one_block(args):
        xb, eb, gb = args
        ub = u[eb]
        hid = jax.nn.gelu(jnp.einsum('td,thkd->thk', xb, ub), approximate=False)
        vb = v[eb]
        return jnp.einsum('thk,thkd->td', (gb * hid).astype(vb.dtype), vb)

    out = lax.map(one_block, (xt.reshape(nb, PEER_TOKEN_BLOCK, D),
                              expert.reshape(nb, PEER_TOKEN_BLOCK, PEER_HEADS, PEER_TOPK),
                              g.reshape(nb, PEER_TOKEN_BLOCK, PEER_HEADS, PEER_TOPK)))
    return out.reshape(B, S, D)


def setup_inputs(seed: int = 0) -> dict:
    key = jax.random.key(seed)
    ks = jax.random.split(key, 20)
    nrm = lambda k, shape, sc: jax.random.normal(k, shape, jnp.float32) * sc
    gain = lambda k, shape: 1.0 + 0.02 * jax.random.normal(k, shape, jnp.float32)
    return {
        'x': nrm(ks[0], (BATCH, SEQ, D_MODEL), 1.0),
        'norm1_g': gain(ks[1], (DEPTH, D_MODEL)),
        'w_in': nrm(ks[2], (DEPTH, D_MODEL, IN_COLS), D_MODEL ** -0.5),
        'a_qnorm_g': gain(ks[3], (DEPTH, HEAD_DIM)),
        'a_knorm_g': gain(ks[4], (DEPTH, HEAD_DIM)),
        'lam_q1': nrm(ks[5], (DEPTH, HEAD_DIM), 0.1),
        'lam_k1': nrm(ks[6], (DEPTH, HEAD_DIM), 0.1),
        'lam_q2': nrm(ks[7], (DEPTH, HEAD_DIM), 0.1),
        'lam_k2': nrm(ks[8], (DEPTH, HEAD_DIM), 0.1),
        'a_subln_g': gain(ks[9], (DEPTH, A_VDIM)),
        'b_qnorm_g': gain(ks[10], (DEPTH, HEAD_DIM)),
        'b_knorm_g': gain(ks[11], (DEPTH, HEAD_DIM)),
        'b_rel_bias': nrm(ks[12], (DEPTH, B_HEADS, 2 * REL_CLIP + 1), 0.1),
        'w_branch': nrm(ks[13], (DEPTH, N_BRANCH, BRANCH_WIDTH, D_MODEL), BRANCH_WIDTH ** -0.5),
        'w_out': nrm(ks[14], (DEPTH, D_MODEL, D_MODEL), D_MODEL ** -0.5),
        'norm2_g': gain(ks[15], (DEPTH, D_MODEL)),
        'peer_wq': nrm(ks[16], (DEPTH, D_MODEL, PEER_HEADS * PEER_QDIM), D_MODEL ** -0.5),
        'peer_subkeys': nrm(ks[17], (DEPTH, 2, PEER_HEADS, N_KEYS, PEER_HALF), PEER_HALF ** -0.5),
        'peer_u': nrm(ks[18], (DEPTH, N_EXPERTS, D_MODEL), D_MODEL ** -0.5),
        'peer_v': nrm(ks[19], (DEPTH, N_EXPERTS, D_MODEL), PEER_HEADS ** -0.5),
    }


def reference(x, norm1_g, w_in, a_qnorm_g, a_knorm_g, lam_q1, lam_k1, lam_q2, lam_k2,
              a_subln_g, b_qnorm_g, b_knorm_g, b_rel_bias, w_branch, w_out, norm2_g,
              peer_wq, peer_subkeys, peer_u, peer_v):
    B, S, D = x.shape
    cos, sin = rope_tables(S)
    for layer in range(DEPTH):
        lam_init = 0.8 - 0.6 * math.exp(-0.3 * layer)
        h = rms_norm(x, norm1_g[layer])
        proj = h @ w_in[layer]
        (aq, ak, av, bq, bk, bv, cq, ck, cv, gl) = jnp.split(proj, SPLIT_POINTS, axis=-1)

        aq = jnp.transpose(aq.reshape(B, S, A_HEADS, 2, HEAD_DIM), (0, 2, 3, 1, 4))
        ak = jnp.transpose(ak.reshape(B, S, A_HEADS, 2, HEAD_DIM), (0, 2, 3, 1, 4))
        av = jnp.transpose(av.reshape(B, S, A_HEADS, A_VDIM), (0, 2, 1, 3))
        aq = apply_partial_rope(rms_norm(aq, a_qnorm_g[layer]), cos, sin)
        ak = apply_partial_rope(rms_norm(ak, a_knorm_g[layer]), cos, sin)
        lam = (jnp.exp(jnp.sum(lam_q1[layer].astype(jnp.float32) * lam_k1[layer].astype(jnp.float32)))
               - jnp.exp(jnp.sum(lam_q2[layer].astype(jnp.float32) * lam_k2[layer].astype(jnp.float32)))
               + lam_init)
        ya = diff_attention(aq, ak, av, lam, a_subln_g[layer], lam_init)
        ya = jnp.transpose(ya, (0, 2, 1, 3)).reshape(B, S, A_V_W)

        to_heads = lambda t, nh: jnp.transpose(t.reshape(B, S, nh, HEAD_DIM), (0, 2, 1, 3))
        bq = rms_norm(to_heads(bq, B_HEADS), b_qnorm_g[layer])
        bk = rms_norm(to_heads(bk, B_HEADS), b_knorm_g[layer])
        yb = chunk_band_attention(bq, bk, to_heads(bv, B_HEADS), b_rel_bias[layer])
        yb = jnp.transpose(yb, (0, 2, 1, 3)).reshape(B, S, B_W)

        yc = stick_breaking_attention(to_heads(cq, C_HEADS), to_heads(ck, C_HEADS), to_heads(cv, C_HEADS))
        yc = jnp.transpose(yc, (0, 2, 1, 3)).reshape(B, S, C_W)

        branches = jnp.stack([ya, yb, yc], axis=2)
        up = jnp.einsum('bsnw,nwd->bsnd', branches, w_branch[layer])
        gates = jax.nn.sigmoid(gl.reshape(B, S, N_BRANCH, D).astype(jnp.float32)).astype(up.dtype)
        merged = jnp.sum(gates * up, axis=2)
        x = x + merged @ w_out[layer]

        h2 = rms_norm(x, norm2_g[layer])
        x = x + peer(h2, peer_wq[layer], peer_subkeys[layer], peer_u[layer], peer_v[layer])
    return x
```

```python
import functools
import math

import numpy as np
import jax
import jax.numpy as jnp
from jax import lax
from jax.experimental import pallas as pl
from jax.experimental.pallas import tpu as pltpu

F32 = jnp.float32
BF16 = jnp.bfloat16

CHUNK = 64
HEAD_DIM = 64
EPS = 1e-6
ROPE_THETA = 500000.0
ROPE_DIMS = HEAD_DIM // 4
A_HEADS = 4
B_HEADS = 8
B_LEFT_CHUNKS = 8
REL_CLIP = 128
C_HEADS = 8
N_BRANCH = 3
PEER_HEADS = 8
N_KEYS = 128
PEER_HALF = 128
PEER_TOPK = 16

LANES = 128
COL_TILE = 512
NEG_BIG = -1e30
VMEM_LIMIT = 56 * 1024 * 1024

GATE_COLS = 3 * 1024
_BLK = lambda tile: (GATE_COLS + tile * COL_TILE) // LANES
AQ_BLK, AK_BLK, AV_BLK = _BLK(0), _BLK(1), _BLK(2)
BQ_BLK, BK_BLK, BV_BLK = _BLK(3), _BLK(4), _BLK(5)
CQ_BLK, CK_BLK, CV_BLK = _BLK(6), _BLK(7), _BLK(8)

_NT = (((1,), (1,)), ((), ()))


def _cparams(sem):
    return pltpu.CompilerParams(dimension_semantics=sem, vmem_limit_bytes=VMEM_LIMIT)


def _inproj_kernel(x_ref, g_ref, w_ref, rc_ref, rs1_ref, rs2_ref, qkg_ref, bd_ref,
                   o_ref, hn_ref):
    j = pl.program_id(1)
    n_gate = GATE_COLS // COL_TILE

    @pl.when(j == 0)
    def _():
        x = x_ref[...]
        ms = jnp.mean(x * x, axis=-1, keepdims=True)
        hn_ref[...] = (x * lax.rsqrt(ms + EPS) * g_ref[...]).astype(BF16)

    acc = jnp.dot(hn_ref[...], w_ref[...], preferred_element_type=F32)

    def seg_norm(y, row):
        sq = y * y
        hi = sq.astype(BF16)
        lo = (sq - hi.astype(F32)).astype(BF16)
        ss = (jnp.dot(hi, bd_ref[...], preferred_element_type=F32)
              + jnp.dot(lo, bd_ref[...], preferred_element_type=F32))
        return y * lax.rsqrt(ss * (1.0 / HEAD_DIM) + EPS) * qkg_ref[pl.ds(row, 1), :]

    def rope(y):
        outs = []
        for c in range(COL_TILE // LANES):
            ys = y[:, c * LANES:(c + 1) * LANES]
            outs.append(ys * rc_ref[...]
                        + pltpu.roll(ys, LANES - ROPE_DIMS // 2, 1) * rs1_ref[...]
                        + pltpu.roll(ys, ROPE_DIMS // 2, 1) * rs2_ref[...])
        return jnp.concatenate(outs, axis=1)

    scale = HEAD_DIM ** -0.5

    @pl.when(j < n_gate)
    def _():
        o_ref[...] = jax.nn.sigmoid(acc).astype(BF16)

    @pl.when(j == n_gate + 0)
    def _():
        o_ref[...] = (rope(seg_norm(acc, 0)) * scale).astype(BF16)

    @pl.when(j == n_gate + 1)
    def _():
        o_ref[...] = rope(seg_norm(acc, 1)).astype(BF16)

    @pl.when(j == n_gate + 3)
    def _():
        o_ref[...] = (seg_norm(acc, 2) * scale).astype(BF16)

    @pl.when(j == n_gate + 4)
    def _():
        o_ref[...] = seg_norm(acc, 3).astype(BF16)

    @pl.when(j == n_gate + 6)
    def _():
        o_ref[...] = (acc * scale).astype(BF16)

    plain = ((j == n_gate + 2) | (j == n_gate + 5) | (j == n_gate + 7) | (j == n_gate + 8))

    @pl.when(plain)
    def _():
        o_ref[...] = acc.astype(BF16)


def _in_projection(x2d, g1, w, rc, rs1, rs2, qkg, bd, seq, tm):
    T, D = x2d.shape
    ncol = w.shape[1]
    nseq = seq // tm
    return pl.pallas_call(
        _inproj_kernel,
        grid=(T // tm, ncol // COL_TILE),
        in_specs=[
            pl.BlockSpec((tm, D), lambda i, j: (i, 0)),
            pl.BlockSpec((1, D), lambda i, j: (0, 0)),
            pl.BlockSpec((D, COL_TILE), lambda i, j: (0, j)),
            pl.BlockSpec((tm, LANES), lambda i, j: (i % nseq, 0)),
            pl.BlockSpec((tm, LANES), lambda i, j: (i % nseq, 0)),
            pl.BlockSpec((tm, LANES), lambda i, j: (i % nseq, 0)),
            pl.BlockSpec((4, COL_TILE), lambda i, j: (0, 0)),
            pl.BlockSpec((COL_TILE, COL_TILE), lambda i, j: (0, 0)),
        ],
        out_specs=pl.BlockSpec((tm, COL_TILE), lambda i, j: (i, j)),
        out_shape=jax.ShapeDtypeStruct((T, ncol), BF16),
        scratch_shapes=[pltpu.VMEM((tm, D), BF16)],
        compiler_params=_cparams(("parallel", "arbitrary")),
        name="in_projection",
    )(x2d, g1, w, rc, rs1, rs2, qkg, bd)


def _diff_attn_kernel(lam_ref, sg_ref, q_ref, k_ref, v_ref, o_ref, m_ref, l_ref, acc_ref,
                      *, tq, lam_init):
    i = pl.program_id(2)
    q = q_ref[...]
    lane = lax.broadcasted_iota(jnp.int32, (tq, LANES), 1)
    zero = jnp.zeros_like(q)
    qs = jnp.concatenate([jnp.where(lane < HEAD_DIM, q, zero),
                          jnp.where(lane >= HEAD_DIM, q, zero)], axis=0)

    m_ref[...] = jnp.full(m_ref.shape, NEG_BIG, F32)
    l_ref[...] = jnp.zeros(l_ref.shape, F32)
    acc_ref[...] = jnp.zeros(acc_ref.shape, F32)

    def step(j, masked):
        k = k_ref[pl.ds(pl.multiple_of(j * tq, tq), tq), :]
        v = v_ref[pl.ds(pl.multiple_of(j * tq, tq), tq), :]
        s = lax.dot_general(qs, k, _NT, preferred_element_type=F32)
        if masked:
            row = lax.broadcasted_iota(jnp.int32, (2 * tq, tq), 0)
            col = lax.broadcasted_iota(jnp.int32, (2 * tq, tq), 1)
            qrow = jnp.where(row >= tq, row - tq, row)
            shift = CHUNK.bit_length() - 1
            s = jnp.where(jnp.right_shift(col, shift) <= jnp.right_shift(qrow, shift),
                          s, NEG_BIG)
        m_prev = m_ref[...]
        m_new = jnp.maximum(m_prev, jnp.max(s, axis=-1, keepdims=True))
        alpha = jnp.exp(m_prev - m_new)
        p = jnp.exp(s - m_new)
        l_ref[...] = alpha * l_ref[...] + jnp.sum(p, axis=-1, keepdims=True)
        acc_ref[...] = alpha * acc_ref[...] + jnp.dot(p.astype(BF16), v,
                                                      preferred_element_type=F32)
        m_ref[...] = m_new

    def body(j, c):
        step(j, False)
        return c

    lax.fori_loop(0, i, body, 0)
    step(i, True)

    lam = (jnp.exp(jnp.sum(lam_ref[0:1, :] * lam_ref[1:2, :], axis=-1, keepdims=True))
           - jnp.exp(jnp.sum(lam_ref[2:3, :] * lam_ref[3:4, :], axis=-1, keepdims=True))
           + lam_init)
    o = acc_ref[...] / l_ref[...]
    o = o[:tq] - lam * o[tq:]
    ms = jnp.mean(o * o, axis=-1, keepdims=True)
    o = o * lax.rsqrt(ms + EPS) * sg_ref[...]
    o_ref[...] = (o * (1.0 - lam_init)).astype(BF16)


def _diff_attention(proj, lamv, subln_g, batch, seq, tq, lam_init):
    T = proj.shape[0]
    nq = seq // tq
    return pl.pallas_call(
        functools.partial(_diff_attn_kernel, tq=tq, lam_init=lam_init),
        grid=(batch, A_HEADS, nq),
        in_specs=[
            pl.BlockSpec((4, HEAD_DIM), lambda b, h, i: (0, 0)),
            pl.BlockSpec((1, LANES), lambda b, h, i: (0, 0)),
            pl.BlockSpec((tq, LANES), lambda b, h, i: (b * nq + i, AQ_BLK + h)),
            pl.BlockSpec((seq, LANES), lambda b, h, i: (b, AK_BLK + h)),
            pl.BlockSpec((seq, LANES), lambda b, h, i: (b, AV_BLK + h)),
        ],
        out_specs=pl.BlockSpec((tq, LANES), lambda b, h, i: (b * nq + i, h)),
        out_shape=jax.ShapeDtypeStruct((T, A_HEADS * LANES), BF16),
        scratch_shapes=[pltpu.VMEM((2 * tq, 1), F32), pltpu.VMEM((2 * tq, 1), F32),
                        pltpu.VMEM((2 * tq, LANES), F32)],
        compiler_params=_cparams(("parallel", "parallel", "arbitrary")),
        name="diff_attention",
    )(lamv, subln_g, proj, proj, proj)


def _band_attn_kernel(bm_ref, q_ref, k_ref, v_ref, o_ref, *, tq):
    i = pl.program_id(2)
    q = q_ref[...]
    lane = lax.broadcasted_iota(jnp.int32, (tq, LANES), 1)
    zero = jnp.zeros_like(q)
    nwin = B_LEFT_CHUNKS * CHUNK // tq + 1
    outs = []
    for half in range(2):
        sel = (lane < HEAD_DIM) if half == 0 else (lane >= HEAD_DIM)
        qh = jnp.where(sel, q, zero)
        ss = []
        for blk in range(nwin):
            kb = i - (nwin - 1) + blk
            kbc = jnp.maximum(kb, 0)
            k = k_ref[pl.ds(pl.multiple_of(kbc * tq, tq), tq), :]
            s = lax.dot_general(qh, k, _NT, preferred_element_type=F32)
            s = s + bm_ref[half, :, blk * tq:(blk + 1) * tq]
            ss.append(jnp.where(kb >= 0, s, NEG_BIG))
        s = jnp.concatenate(ss, axis=1)
        m = jnp.max(s, axis=-1, keepdims=True)
        p = jnp.exp(s - m)
        l = jnp.sum(p, axis=-1, keepdims=True)
        pb = p.astype(BF16)
        o = jnp.zeros((tq, LANES), F32)
        for blk in range(nwin):
            kbc = jnp.maximum(i - (nwin - 1) + blk, 0)
            v = v_ref[pl.ds(pl.multiple_of(kbc * tq, tq), tq), :]
            o = o + jnp.dot(pb[:, blk * tq:(blk + 1) * tq], v, preferred_element_type=F32)
        outs.append(o / l)
    o_ref[...] = jnp.where(lane < HEAD_DIM, outs[0], outs[1]).astype(BF16)


def _band_attention(proj, biasmask, batch, seq, tq):
    T = proj.shape[0]
    nq = seq // tq
    npair = B_HEADS // 2
    wwin = biasmask.shape[-1]
    return pl.pallas_call(
        functools.partial(_band_attn_kernel, tq=tq),
        grid=(batch, npair, nq),
        in_specs=[
            pl.BlockSpec((2, tq, wwin), lambda b, h, i: (h, 0, 0)),
            pl.BlockSpec((tq, LANES), lambda b, h, i: (b * nq + i, BQ_BLK + h)),
            pl.BlockSpec((seq, LANES), lambda b, h, i: (b, BK_BLK + h)),
            pl.BlockSpec((seq, LANES), lambda b, h, i: (b, BV_BLK + h)),
        ],
        out_specs=pl.BlockSpec((tq, LANES), lambda b, h, i: (b * nq + i, h)),
        out_shape=jax.ShapeDtypeStruct((T, npair * LANES), BF16),
        compiler_params=_cparams(("parallel", "parallel", "arbitrary")),
        name="band_attention",
    )(biasmask, proj, proj, proj)


def _stick_attn_kernel(tri_ref, q_ref, k_ref, v_ref, o_ref, carry_ref, acc_ref, *, tq):
    i = pl.program_id(2)
    q = q_ref[...]
    lane = lax.broadcasted_iota(jnp.int32, (tq, LANES), 1)
    zero = jnp.zeros_like(q)
    qh = (jnp.where(lane < HEAD_DIM, q, zero), jnp.where(lane >= HEAD_DIM, q, zero))

    carry_ref[...] = jnp.zeros(carry_ref.shape, F32)
    acc_ref[...] = jnp.zeros(acc_ref.shape, F32)

    def step(j, masked):
        k = k_ref[pl.ds(pl.multiple_of(j * tq, tq), tq), :]
        v = v_ref[pl.ds(pl.multiple_of(j * tq, tq), tq), :]
        if masked:
            row = lax.broadcasted_iota(jnp.int32, (tq, tq), 0)
            col = lax.broadcasted_iota(jnp.int32, (tq, tq), 1)
            strict = col < row
        for half in range(2):
            z = lax.dot_general(qh[half], k, _NT, preferred_element_type=F32)
            sp = jnp.maximum(z, 0.0) + jnp.log(1.0 + jnp.exp(-jnp.abs(z)))
            lk = -sp
            if masked:
                lk = jnp.where(strict, lk, 0.0)
            later = jnp.dot(lk.astype(BF16), tri_ref[...], preferred_element_type=F32)
            carry = carry_ref[half]
            a = jnp.exp((z - sp) + (later + carry))
            if masked:
                a = jnp.where(strict, a, 0.0)
            acc_ref[half] = acc_ref[half] + jnp.dot(a.astype(BF16), v,
                                                    preferred_element_type=F32)
            carry_ref[half] = carry + (later[:, 0:1] + lk[:, 0:1])

    step(i, True)

    def body(t, c):
        step(i - 1 - t, False)
        return c

    lax.fori_loop(0, i, body, 0)
    o_ref[...] = jnp.where(lane < HEAD_DIM, acc_ref[0], acc_ref[1]).astype(BF16)


def _stick_attention(proj, tri, batch, seq, tq):
    T = proj.shape[0]
    nq = seq // tq
    npair = C_HEADS // 2
    return pl.pallas_call(
        functools.partial(_stick_attn_kernel, tq=tq),
        grid=(batch, npair, nq),
        in_specs=[
            pl.BlockSpec((tq, tq), lambda b, h, i: (0, 0)),
            pl.BlockSpec((tq, LANES), lambda b, h, i: (b * nq + i, CQ_BLK + h)),
            pl.BlockSpec((seq, LANES), lambda b, h, i: (b, CK_BLK + h)),
            pl.BlockSpec((seq, LANES), lambda b, h, i: (b, CV_BLK + h)),
        ],
        out_specs=pl.BlockSpec((tq, LANES), lambda b, h, i: (b * nq + i, h)),
        out_shape=jax.ShapeDtypeStruct((T, npair * LANES), BF16),
        scratch_shapes=[pltpu.VMEM((2, tq, 1), F32), pltpu.VMEM((2, tq, LANES), F32)],
        compiler_params=_cparams(("parallel", "parallel", "arbitrary")),
        name="stick_attention",
    )(tri, proj, proj, proj)


def _merge_kernel(x_ref, ya_ref, yb_ref, yc_ref, g0_ref, g1_ref, g2_ref, wb_ref, wo_ref,
                  n2_ref, xo_ref, h2_ref):
    merged = None
    for n, (y_ref, g_ref) in enumerate(((ya_ref, g0_ref), (yb_ref, g1_ref), (yc_ref, g2_ref))):
        up = jnp.dot(y_ref[...], wb_ref[n], preferred_element_type=F32)
        term = g_ref[...].astype(F32) * up
        merged = term if merged is None else merged + term
    xn = x_ref[...] + jnp.dot(merged.astype(BF16), wo_ref[...], preferred_element_type=F32)
    xo_ref[...] = xn
    ms = jnp.mean(xn * xn, axis=-1, keepdims=True)
    h2_ref[...] = (xn * lax.rsqrt(ms + EPS) * n2_ref[...]).astype(BF16)


def _merge(x2d, ya, yb, yc, proj, wb, wo, n2, tm):
    T, D = x2d.shape
    W = ya.shape[1]
    yspec = pl.BlockSpec((tm, W), lambda i: (i, 0))
    return pl.pallas_call(
        _merge_kernel,
        grid=(T // tm,),
        in_specs=[
            pl.BlockSpec((tm, D), lambda i: (i, 0)),
            yspec, yspec, yspec,
            pl.BlockSpec((tm, D), lambda i: (i, 0)),
            pl.BlockSpec((tm, D), lambda i: (i, 1)),
            pl.BlockSpec((tm, D), lambda i: (i, 2)),
            pl.BlockSpec((N_BRANCH, W, D), lambda i: (0, 0, 0)),
            pl.BlockSpec((D, D), lambda i: (0, 0)),
            pl.BlockSpec((1, D), lambda i: (0, 0)),
        ],
        out_specs=[pl.BlockSpec((tm, D), lambda i: (i, 0)),
                   pl.BlockSpec((tm, D), lambda i: (i, 0))],
        out_shape=[jax.ShapeDtypeStruct((T, D), F32), jax.ShapeDtypeStruct((T, D), BF16)],
        compiler_params=_cparams(("parallel",)),
        name="gated_merge",
    )(x2d, ya, yb, yc, proj, proj, proj, wb, wo, n2)


def _candidate_pairs():
    return [(a, b) for a in range(PEER_TOPK) for b in range(PEER_TOPK)
            if (a + 1) * (b + 1) <= PEER_TOPK]


def _peer_route_kernel(h2_ref, wqt_ref, sk_ref, c1_ref, r2_ref, e1_ref, e2_ref,
                       s_ref, top_ref):
    qT = lax.dot_general(wqt_ref[...], h2_ref[...], _NT,
                         preferred_element_type=F32).astype(BF16)
    kio = lax.broadcasted_iota(jnp.int32, (N_KEYS, LANES), 0).astype(F32)
    pairs = _candidate_pairs()

    ranks = {}
    for h in range(PEER_HEADS):
        for p in range(2):
            r0 = (h * 2 + p) * PEER_HALF
            x0 = jnp.dot(sk_ref[p, h], qT[r0:r0 + PEER_HALF, :],
                         preferred_element_type=F32)
            s_ref[p, h] = x0

            def it(r, xc, p=p, h=h):
                x, rank = xc
                m = jnp.max(x, axis=0, keepdims=True)
                idx = jnp.where(x == m, kio, float(N_KEYS))
                first = jnp.min(idx, axis=0, keepdims=True)
                hit = kio == first
                top_ref[p, r, pl.ds(h, 1), :] = m
                return (jnp.where(hit, -jnp.inf, x),
                        jnp.where(hit, r.astype(F32), rank))

            _, rank = lax.fori_loop(0, PEER_TOPK, it,
                                    (x0, jnp.full((N_KEYS, LANES), float(PEER_TOPK), F32)))
            if p == 0:
                ranks[h] = rank
            else:
                r2_ref[h] = rank

    a = [top_ref[0, r] for r in range(PEER_TOPK)]
    b = [top_ref[1, r] for r in range(PEER_TOPK)]
    cand = [a[ia] + b[ib] for (ia, ib) in pairs]
    m0 = cand[0]
    cnt = [jnp.zeros_like(m0) for _ in range(PEER_TOPK)]
    z = jnp.zeros_like(m0)
    for _ in range(PEER_TOPK):
        m = functools.reduce(jnp.maximum, cand)
        z = z + jnp.exp(m - m0)
        found = jnp.zeros(m0.shape, jnp.bool_)
        for ci, (ia, ib) in enumerate(pairs):
            hit = (cand[ci] == m) & jnp.logical_not(found)
            found = found | hit
            cand[ci] = jnp.where(hit, -jnp.inf, cand[ci])
            cnt[ia] = cnt[ia] + jnp.where(hit, 1.0, 0.0)
    inv_z = 1.0 / z

    for h in range(PEER_HEADS):
        c1 = jnp.zeros((N_KEYS, LANES), F32)
        for r in range(PEER_TOPK):
            c1 = jnp.where(ranks[h] == float(r), cnt[r][h:h + 1, :], c1)
        c1_ref[h] = c1
        e1_ref[h] = jnp.exp(s_ref[0, h] - a[0][h:h + 1, :])
        e2_ref[h] = jnp.exp(s_ref[1, h] - b[0][h:h + 1, :]) * inv_z[h:h + 1, :]


def _peer_route(h2, wqt, sk):
    T, D = h2.shape
    oshape = jax.ShapeDtypeStruct((PEER_HEADS, N_KEYS, T), F32)
    ospec = pl.BlockSpec((PEER_HEADS, N_KEYS, LANES), lambda i: (0, 0, i))
    return pl.pallas_call(
        _peer_route_kernel,
        grid=(T // LANES,),
        in_specs=[
            pl.BlockSpec((LANES, D), lambda i: (i, 0)),
            pl.BlockSpec(wqt.shape, lambda i: (0, 0)),
            pl.BlockSpec(sk.shape, lambda i: (0, 0, 0, 0)),
        ],
        out_specs=[ospec, ospec, ospec, ospec],
        out_shape=[oshape, oshape, oshape, oshape],
        scratch_shapes=[pltpu.VMEM((2, PEER_HEADS, N_KEYS, LANES), F32),
                        pltpu.VMEM((2, PEER_TOPK, PEER_HEADS, LANES), F32)],
        compiler_params=_cparams(("parallel",)),
        name="peer_route",
    )(h2, wqt, sk)


def _peer_dense_kernel(x_ref, h2_ref, u_ref, vt_ref, c1_ref, r2_ref, e1_ref, e2_ref,
                       o_ref, acc_ref, g_ref, *, tm, ce):
    c = pl.program_id(1)

    @pl.when(c == 0)
    def _():
        acc_ref[...] = jnp.zeros(acc_ref.shape, F32)

    hT = lax.dot_general(u_ref[...], h2_ref[...], _NT, preferred_element_type=F32)
    n_i1 = ce // N_KEYS
    for ii in range(n_i1):
        w = jnp.zeros((N_KEYS, tm), F32)
        for h in range(PEER_HEADS):
            c1row = c1_ref[h, ii:ii + 1, :]
            e1row = e1_ref[h, ii:ii + 1, :]
            w = w + jnp.where(r2_ref[h] < c1row, e1row * e2_ref[h], 0.0)
        hh = hT[ii * N_KEYS:(ii + 1) * N_KEYS, :]
        act = 0.5 * hh * (1.0 + lax.erf(hh * (1.0 / math.sqrt(2.0))))
        g_ref[ii * N_KEYS:(ii + 1) * N_KEYS, :] = (w * act).astype(BF16)
    acc_ref[...] += jnp.dot(vt_ref[...], g_ref[...], preferred_element_type=F32)

    @pl.when(c == pl.num_programs(1) - 1)
    def _():
        o_ref[...] = x_ref[...] + acc_ref[...].T


def _peer_dense(x2d, h2, u, vt, c1, r2, e1, e2, tm, ce):
    T, D = x2d.shape
    ne = u.shape[0]
    rspec = pl.BlockSpec((PEER_HEADS, N_KEYS, tm), lambda i, c: (0, 0, i))
    cspec = pl.BlockSpec((PEER_HEADS, ce // N_KEYS, tm), lambda i, c: (0, c, i))
    return pl.pallas_call(
        functools.partial(_peer_dense_kernel, tm=tm, ce=ce),
        grid=(T // tm, ne // ce),
        in_specs=[
            pl.BlockSpec((tm, D), lambda i, c: (i, 0)),
            pl.BlockSpec((tm, D), lambda i, c: (i, 0)),
            pl.BlockSpec((ce, D), lambda i, c: (c, 0)),
            pl.BlockSpec((D, ce), lambda i, c: (0, c)),
            cspec, rspec, cspec, rspec,
        ],
        out_specs=pl.BlockSpec((tm, D), lambda i, c: (i, 0)),
        out_shape=jax.ShapeDtypeStruct((T, D), F32),
        scratch_shapes=[pltpu.VMEM((D, tm), F32), pltpu.VMEM((ce, tm), BF16)],
        compiler_params=_cparams(("parallel", "arbitrary")),
        name="peer_dense",
    )(x2d, h2, u, vt, c1, r2, e1, e2)


def _rope_tables(seq):
    inv = 1.0 / (ROPE_THETA ** (jnp.arange(0, ROPE_DIMS, 2, dtype=F32) / ROPE_DIMS))
    ang = jnp.arange(seq, dtype=F32)[:, None] * inv[None, :]
    cos, sin = jnp.cos(ang), jnp.sin(ang)
    half = ROPE_DIMS // 2
    pad = HEAD_DIM - ROPE_DIMS
    one = jnp.ones((seq, pad), F32)
    zer = jnp.zeros((seq, pad), F32)
    zh = jnp.zeros((seq, half), F32)
    rc = jnp.concatenate([cos, cos, one], axis=1)
    rs1 = jnp.concatenate([-sin, zh, zer], axis=1)
    rs2 = jnp.concatenate([zh, sin, zer], axis=1)
    rep = LANES // HEAD_DIM
    return jnp.tile(rc, (1, rep)), jnp.tile(rs1, (1, rep)), jnp.tile(rs2, (1, rep))


def _band_bias_mask(rel_bias, tq):
    nwin = B_LEFT_CHUNKS * CHUNK // tq + 1
    r = np.arange(tq)[:, None]
    c = np.arange(nwin * tq)[None, :]
    kchunk = c // CHUNK - (nwin - 1) * (tq // CHUNK)
    qchunk = r // CHUNK
    valid = (kchunk <= qchunk) & (kchunk >= qchunk - B_LEFT_CHUNKS)
    dist = r - (c - (nwin - 1) * tq)
    idx = np.clip(dist, -REL_CLIP, REL_CLIP) + REL_CLIP
    bias = rel_bias[:, idx].astype(F32)
    return jnp.where(jnp.asarray(valid)[None], bias, NEG_BIG)


def _layer(x2d, p, layer, batch, seq, consts):
    T, D = x2d.shape
    lam_init = 0.8 - 0.6 * math.exp(-0.3 * layer)
    tq = consts["tq"]

    w_in = p["w_in"]
    n_qkv = w_in.shape[1] - GATE_COLS
    w_perm = jnp.concatenate([w_in[:, n_qkv:], w_in[:, :n_qkv]], axis=1).astype(BF16)
    tile_g = lambda g: jnp.tile(g.astype(F32), COL_TILE // HEAD_DIM)
    qkg = jnp.stack([tile_g(p["a_qnorm_g"]), tile_g(p["a_knorm_g"]),
                     tile_g(p["b_qnorm_g"]), tile_g(p["b_knorm_g"])])
    proj = _in_projection(x2d, p["norm1_g"].reshape(1, D).astype(F32), w_perm,
                          consts["rc"], consts["rs1"], consts["rs2"], qkg, consts["bd"],
                          seq, consts["tm_in"])

    lamv = jnp.stack([p["lam_q1"], p["lam_k1"], p["lam_q2"], p["lam_k2"]]).astype(F32)
    ya = _diff_attention(proj, lamv, p["a_subln_g"].reshape(1, LANES).astype(F32),
                         batch, seq, tq, lam_init)
    yb = _band_attention(proj, _band_bias_mask(p["b_rel_bias"], tq), batch, seq, tq)
    yc = _stick_attention(proj, consts["tri"], batch, seq, tq)

    xn, h2 = _merge(x2d, ya, yb, yc, proj, p["w_branch"].astype(BF16),
                    p["w_out"].astype(BF16), p["norm2_g"].reshape(1, D).astype(F32),
                    consts["tm_merge"])

    wqt = p["peer_wq"].T.astype(BF16)
    sk = p["peer_subkeys"].astype(BF16)
    c1, r2, e1, e2 = _peer_route(h2, wqt, sk)
    u = p["peer_u"].astype(BF16)
    vt = p["peer_v"].T.astype(BF16)
    return _peer_dense(xn, h2, u, vt, c1, r2, e1, e2, consts["tm_peer"], consts["ce"])


def kernel(x, norm1_g, w_in, a_qnorm_g, a_knorm_g, lam_q1, lam_k1, lam_q2, lam_k2, a_subln_g, b_qnorm_g, b_knorm_g, b_rel_bias, w_branch, w_out, norm2_g, peer_wq, peer_subkeys, peer_u, peer_v):
    B, S, D = x.shape
    T = B * S
    depth = w_in.shape[0]
    tq = min(256, S)
    rc, rs1, rs2 = _rope_tables(S)
    seg = np.arange(COL_TILE) // HEAD_DIM
    bd = jnp.asarray(seg[:, None] == seg[None, :], BF16)
    kk = np.arange(tq)
    tri = jnp.asarray(kk[:, None] > kk[None, :], BF16)
    consts = dict(tq=tq, rc=rc, rs1=rs1, rs2=rs2, bd=bd, tri=tri,
                  tm_in=min(512, S), tm_merge=min(512, T), tm_peer=min(512, T), ce=1024)
    params = dict(norm1_g=norm1_g, w_in=w_in, a_qnorm_g=a_qnorm_g, a_knorm_g=a_knorm_g,
                  lam_q1=lam_q1, lam_k1=lam_k1, lam_q2=lam_q2, lam_k2=lam_k2,
                  a_subln_g=a_subln_g, b_qnorm_g=b_qnorm_g, b_knorm_g=b_knorm_g,
                  b_rel_bias=b_rel_bias, w_branch=w_branch, w_out=w_out, norm2_g=norm2_g,
                  peer_wq=peer_wq, peer_subkeys=peer_subkeys, peer_u=peer_u, peer_v=peer_v)
    x2d = x.reshape(T, D)
    for layer in range(depth):
        x2d = _layer(x2d, {k: v[layer] for k, v in params.items()}, layer, B, S, consts)
    return x2d.reshape(B, S, D)
```

```python
import functools
import math

import numpy as np
import jax
import jax.numpy as jnp
from jax import lax
from jax.experimental import pallas as pl
from jax.experimental.pallas import tpu as pltpu

F32 = jnp.float32
BF16 = jnp.bfloat16

CHUNK = 64
HEAD_DIM = 64
EPS = 1e-6
ROPE_THETA = 500000.0
ROPE_DIMS = HEAD_DIM // 4
A_HEADS = 4
B_HEADS = 8
B_LEFT_CHUNKS = 8
REL_CLIP = 128
C_HEADS = 8
N_BRANCH = 3
PEER_HEADS = 8
N_KEYS = 128
PEER_HALF = 128
PEER_TOPK = 16

LANES = 128
COL_TILE = 512
NEG_BIG = -1e30
LOG2E = math.log2(math.e)
VMEM_LIMIT = 56 * 1024 * 1024

GATE_COLS = 3 * 1024
_BLK = lambda tile: (GATE_COLS + tile * COL_TILE) // LANES
AQ_BLK, AK_BLK, AV_BLK = _BLK(0), _BLK(1), _BLK(2)
BQ_BLK, BK_BLK, BV_BLK = _BLK(3), _BLK(4), _BLK(5)
CQ_BLK, CK_BLK, CV_BLK = _BLK(6), _BLK(7), _BLK(8)

_NT = (((1,), (1,)), ((), ()))


def _cparams(sem):
    return pltpu.CompilerParams(dimension_semantics=sem, vmem_limit_bytes=VMEM_LIMIT)


def _inproj_kernel(x_ref, g_ref, w_ref, rc_ref, rs1_ref, rs2_ref, qkg_ref, bd_ref,
                   o_ref, hn_ref):
    j = pl.program_id(1)
    n_gate = GATE_COLS // COL_TILE

    @pl.when(j == 0)
    def _():
        x = x_ref[...]
        ms = jnp.mean(x * x, axis=-1, keepdims=True)
        hn_ref[...] = (x * lax.rsqrt(ms + EPS) * g_ref[...]).astype(BF16)

    acc = jnp.dot(hn_ref[...], w_ref[...], preferred_element_type=F32)

    def seg_norm(y, row):
        sq = y * y
        hi = sq.astype(BF16)
        lo = (sq - hi.astype(F32)).astype(BF16)
        ss = (jnp.dot(hi, bd_ref[...], preferred_element_type=F32)
              + jnp.dot(lo, bd_ref[...], preferred_element_type=F32))
        return y * lax.rsqrt(ss * (1.0 / HEAD_DIM) + EPS) * qkg_ref[pl.ds(row, 1), :]

    def rope(y):
        outs = []
        for c in range(COL_TILE // LANES):
            ys = y[:, c * LANES:(c + 1) * LANES]
            outs.append(ys * rc_ref[...]
                        + pltpu.roll(ys, LANES - ROPE_DIMS // 2, 1) * rs1_ref[...]
                        + pltpu.roll(ys, ROPE_DIMS // 2, 1) * rs2_ref[...])
        return jnp.concatenate(outs, axis=1)

    scale = HEAD_DIM ** -0.5 * LOG2E

    @pl.when(j < n_gate)
    def _():
        o_ref[...] = jax.nn.sigmoid(acc).astype(BF16)

    @pl.when(j == n_gate + 0)
    def _():
        o_ref[...] = (rope(seg_norm(acc, 0)) * scale).astype(BF16)

    @pl.when(j == n_gate + 1)
    def _():
        o_ref[...] = rope(seg_norm(acc, 1)).astype(BF16)

    @pl.when(j == n_gate + 3)
    def _():
        o_ref[...] = (seg_norm(acc, 2) * scale).astype(BF16)

    @pl.when(j == n_gate + 4)
    def _():
        o_ref[...] = seg_norm(acc, 3).astype(BF16)

    @pl.when(j == n_gate + 6)
    def _():
        o_ref[...] = (acc * scale).astype(BF16)

    plain = ((j == n_gate + 2) | (j == n_gate + 5) | (j == n_gate + 7) | (j == n_gate + 8))

    @pl.when(plain)
    def _():
        o_ref[...] = acc.astype(BF16)


def _in_projection(x2d, g1, w, rc, rs1, rs2, qkg, bd, seq, tm):
    T, D = x2d.shape
    ncol = w.shape[1]
    nseq = seq // tm
    return pl.pallas_call(
        _inproj_kernel,
        grid=(T // tm, ncol // COL_TILE),
        in_specs=[
            pl.BlockSpec((tm, D), lambda i, j: (i, 0)),
            pl.BlockSpec((1, D), lambda i, j: (0, 0)),
            pl.BlockSpec((D, COL_TILE), lambda i, j: (0, j)),
            pl.BlockSpec((tm, LANES), lambda i, j: (i % nseq, 0)),
            pl.BlockSpec((tm, LANES), lambda i, j: (i % nseq, 0)),
            pl.BlockSpec((tm, LANES), lambda i, j: (i % nseq, 0)),
            pl.BlockSpec((4, COL_TILE), lambda i, j: (0, 0)),
            pl.BlockSpec((COL_TILE, COL_TILE), lambda i, j: (0, 0)),
        ],
        out_specs=pl.BlockSpec((tm, COL_TILE), lambda i, j: (i, j)),
        out_shape=jax.ShapeDtypeStruct((T, ncol), BF16),
        scratch_shapes=[pltpu.VMEM((tm, D), BF16)],
        compiler_params=_cparams(("parallel", "arbitrary")),
        name="in_projection",
    )(x2d, g1, w, rc, rs1, rs2, qkg, bd)


def _diff_attn_kernel(lam_ref, sg_ref, q_ref, k_ref, v_ref, o_ref, m_ref, l_ref, acc_ref,
                      *, tq, lam_init):
    i = pl.program_id(2)
    q = q_ref[...]
    lane = lax.broadcasted_iota(jnp.int32, (tq, LANES), 1)
    zero = jnp.zeros_like(q)
    qs = jnp.concatenate([jnp.where(lane < HEAD_DIM, q, zero),
                          jnp.where(lane >= HEAD_DIM, q, zero)], axis=0)

    m_ref[...] = jnp.full(m_ref.shape, NEG_BIG, F32)
    l_ref[...] = jnp.zeros(l_ref.shape, F32)
    acc_ref[...] = jnp.zeros(acc_ref.shape, F32)

    def step(j, masked):
        k = k_ref[pl.ds(pl.multiple_of(j * tq, tq), tq), :]
        v = v_ref[pl.ds(pl.multiple_of(j * tq, tq), tq), :]
        s = lax.dot_general(qs, k, _NT, preferred_element_type=F32)
        if masked:
            row = lax.broadcasted_iota(jnp.int32, (2 * tq, tq), 0)
            col = lax.broadcasted_iota(jnp.int32, (2 * tq, tq), 1)
            qrow = jnp.where(row >= tq, row - tq, row)
            shift = CHUNK.bit_length() - 1
            s = jnp.where(jnp.right_shift(col, shift) <= jnp.right_shift(qrow, shift),
                          s, NEG_BIG)
        m_prev = m_ref[...]
        m_new = jnp.maximum(m_prev, jnp.max(s, axis=-1, keepdims=True))
        alpha = jnp.exp2(m_prev - m_new)
        ps = [jnp.exp2(s[:, c * LANES:(c + 1) * LANES] - m_new) for c in range(tq // LANES)]
        l_ref[...] = alpha * l_ref[...] + jnp.sum(functools.reduce(jnp.add, ps),
                                                  axis=-1, keepdims=True)
        p = jnp.concatenate(ps, axis=1).astype(BF16)
        acc_ref[...] = alpha * acc_ref[...] + jnp.dot(p, v, preferred_element_type=F32)
        m_ref[...] = m_new

    def body(j, c):
        step(j, False)
        return c

    lax.fori_loop(0, i, body, 0)
    step(i, True)

    lam = (jnp.exp(jnp.sum(lam_ref[0:1, :] * lam_ref[1:2, :], axis=-1, keepdims=True))
           - jnp.exp(jnp.sum(lam_ref[2:3, :] * lam_ref[3:4, :], axis=-1, keepdims=True))
           + lam_init)
    o = acc_ref[...] / l_ref[...]
    o = o[:tq] - lam * o[tq:]
    ms = jnp.mean(o * o, axis=-1, keepdims=True)
    o = o * lax.rsqrt(ms + EPS) * sg_ref[...]
    o_ref[...] = (o * (1.0 - lam_init)).astype(BF16)


def _diff_attention(proj, lamv, subln_g, batch, seq, tq, lam_init):
    T = proj.shape[0]
    nq = seq // tq
    return pl.pallas_call(
        functools.partial(_diff_attn_kernel, tq=tq, lam_init=lam_init),
        grid=(batch, A_HEADS, nq),
        in_specs=[
            pl.BlockSpec((4, HEAD_DIM), lambda b, h, i: (0, 0)),
            pl.BlockSpec((1, LANES), lambda b, h, i: (0, 0)),
            pl.BlockSpec((tq, LANES), lambda b, h, i: (b * nq + i, AQ_BLK + h)),
            pl.BlockSpec((seq, LANES), lambda b, h, i: (b, AK_BLK + h)),
            pl.BlockSpec((seq, LANES), lambda b, h, i: (b, AV_BLK + h)),
        ],
        out_specs=pl.BlockSpec((tq, LANES), lambda b, h, i: (b * nq + i, h)),
        out_shape=jax.ShapeDtypeStruct((T, A_HEADS * LANES), BF16),
        scratch_shapes=[pltpu.VMEM((2 * tq, LANES), F32), pltpu.VMEM((2 * tq, LANES), F32),
                        pltpu.VMEM((2 * tq, LANES), F32)],
        compiler_params=_cparams(("parallel", "parallel", "arbitrary")),
        name="diff_attention",
    )(lamv, subln_g, proj, proj, proj)


def _band_attn_kernel(bm_ref, q_ref, k_ref, v_ref, o_ref, *, tq):
    i = pl.program_id(2)
    q = q_ref[...]
    lane = lax.broadcasted_iota(jnp.int32, (tq, LANES), 1)
    zero = jnp.zeros_like(q)
    nwin = B_LEFT_CHUNKS * CHUNK // tq + 1
    outs = []
    for half in range(2):
        sel = (lane < HEAD_DIM) if half == 0 else (lane >= HEAD_DIM)
        qh = jnp.where(sel, q, zero)
        ss = []
        for blk in range(nwin):
            kb = i - (nwin - 1) + blk
            kbc = jnp.maximum(kb, 0)
            k = k_ref[pl.ds(pl.multiple_of(kbc * tq, tq), tq), :]
            s = lax.dot_general(qh, k, _NT, preferred_element_type=F32)
            s = s + bm_ref[half, :, blk * tq:(blk + 1) * tq]
            ss.append(jnp.where(kb >= 0, s, NEG_BIG))
        s = jnp.concatenate(ss, axis=1)
        m = jnp.max(s, axis=-1, keepdims=True)
        p = jnp.exp2(s - m)
        l = jnp.sum(p, axis=-1, keepdims=True)
        pb = p.astype(BF16)
        o = jnp.zeros((tq, LANES), F32)
        for blk in range(nwin):
            kbc = jnp.maximum(i - (nwin - 1) + blk, 0)
            v = v_ref[pl.ds(pl.multiple_of(kbc * tq, tq), tq), :]
            o = o + jnp.dot(pb[:, blk * tq:(blk + 1) * tq], v, preferred_element_type=F32)
        outs.append(o / l)
    o_ref[...] = jnp.where(lane < HEAD_DIM, outs[0], outs[1]).astype(BF16)


def _band_attention(proj, biasmask, batch, seq, tq):
    T = proj.shape[0]
    nq = seq // tq
    npair = B_HEADS // 2
    wwin = biasmask.shape[-1]
    return pl.pallas_call(
        functools.partial(_band_attn_kernel, tq=tq),
        grid=(batch, npair, nq),
        in_specs=[
            pl.BlockSpec((2, tq, wwin), lambda b, h, i: (h, 0, 0)),
            pl.BlockSpec((tq, LANES), lambda b, h, i: (b * nq + i, BQ_BLK + h)),
            pl.BlockSpec((seq, LANES), lambda b, h, i: (b, BK_BLK + h)),
            pl.BlockSpec((seq, LANES), lambda b, h, i: (b, BV_BLK + h)),
        ],
        out_specs=pl.BlockSpec((tq, LANES), lambda b, h, i: (b * nq + i, h)),
        out_shape=jax.ShapeDtypeStruct((T, npair * LANES), BF16),
        compiler_params=_cparams(("parallel", "parallel", "arbitrary")),
        name="band_attention",
    )(biasmask, proj, proj, proj)


def _stick_attn_kernel(tri_ref, q_ref, k_ref, v_ref, o_ref, carry_ref, acc_ref, *, tq):
    i = pl.program_id(2)
    q = q_ref[...]
    lane = lax.broadcasted_iota(jnp.int32, (tq, LANES), 1)
    zero = jnp.zeros_like(q)
    qh = (jnp.where(lane < HEAD_DIM, q, zero), jnp.where(lane >= HEAD_DIM, q, zero))

    carry_ref[...] = jnp.zeros(carry_ref.shape, F32)
    acc_ref[...] = jnp.zeros(acc_ref.shape, F32)

    nt = tq // LANES
    lane_tiles = lambda t: [t[:, c * LANES:(c + 1) * LANES] for c in range(nt)]

    def step(blocks, masked):
        if masked:
            row = lax.broadcasted_iota(jnp.int32, (tq, tq), 0)
            col = lax.broadcasted_iota(jnp.int32, (tq, tq), 1)
            strict = col < row
        for half in range(2):
            parts = []
            for j in blocks:
                k = k_ref[pl.ds(pl.multiple_of(j * tq, tq), tq), :]
                z2 = lax.dot_general(qh[half], k, _NT, preferred_element_type=F32)
                nz2 = -z2
                lg2 = jnp.log(1.0 + jnp.exp2(jnp.minimum(z2, nz2))) * LOG2E
                lk = jnp.minimum(nz2, 0.0) - lg2
                ls = lk + z2
                if masked:
                    lk = jnp.where(strict, lk, 0.0)
                later = jnp.dot(lk.astype(BF16), tri_ref[...], preferred_element_type=F32)
                tot = jnp.sum(functools.reduce(jnp.add, lane_tiles(lk)),
                              axis=-1, keepdims=True)
                parts.append((j, ls, later, tot))
            carry = carry_ref[half]
            acc = acc_ref[half]
            for j, ls, later, tot in parts:
                a = jnp.concatenate([jnp.exp2(x + (y + carry))
                                     for x, y in zip(lane_tiles(ls), lane_tiles(later))], axis=1)
                if masked:
                    a = jnp.where(strict, a, 0.0)
                v = v_ref[pl.ds(pl.multiple_of(j * tq, tq), tq), :]
                acc = acc + jnp.dot(a.astype(BF16), v, preferred_element_type=F32)
                carry = carry + tot
            acc_ref[half] = acc
            carry_ref[half] = carry

    step([i], True)

    def pair_body(t, c):
        j = i - 1 - 2 * t
        step([j, j - 1], False)
        return c

    lax.fori_loop(0, i // 2, pair_body, 0)

    @pl.when(i % 2 == 1)
    def _():
        step([0], False)
    o_ref[...] = jnp.where(lane < HEAD_DIM, acc_ref[0], acc_ref[1]).astype(BF16)


def _stick_attention(proj, tri, batch, seq, tq):
    T = proj.shape[0]
    nq = seq // tq
    npair = C_HEADS // 2
    return pl.pallas_call(
        functools.partial(_stick_attn_kernel, tq=tq),
        grid=(batch, npair, nq),
        in_specs=[
            pl.BlockSpec((tq, tq), lambda b, h, i: (0, 0)),
            pl.BlockSpec((tq, LANES), lambda b, h, i: (b * nq + i, CQ_BLK + h)),
            pl.BlockSpec((seq, LANES), lambda b, h, i: (b, CK_BLK + h)),
            pl.BlockSpec((seq, LANES), lambda b, h, i: (b, CV_BLK + h)),
        ],
        out_specs=pl.BlockSpec((tq, LANES), lambda b, h, i: (b * nq + i, h)),
        out_shape=jax.ShapeDtypeStruct((T, npair * LANES), BF16),
        scratch_shapes=[pltpu.VMEM((2, tq, LANES), F32), pltpu.VMEM((2, tq, LANES), F32)],
        compiler_params=_cparams(("parallel", "parallel", "arbitrary")),
        name="stick_attention",
    )(tri, proj, proj, proj)


def _merge_kernel(x_ref, ya_ref, yb_ref, yc_ref, g0_ref, g1_ref, g2_ref, wb_ref, wo_ref,
                  n2_ref, xo_ref, h2_ref):
    merged = None
    for n, (y_ref, g_ref) in enumerate(((ya_ref, g0_ref), (yb_ref, g1_ref), (yc_ref, g2_ref))):
        up = jnp.dot(y_ref[...], wb_ref[n], preferred_element_type=F32)
        term = g_ref[...].astype(F32) * up
        merged = term if merged is None else merged + term
    xn = x_ref[...] + jnp.dot(merged.astype(BF16), wo_ref[...], preferred_element_type=F32)
    xo_ref[...] = xn
    ms = jnp.mean(xn * xn, axis=-1, keepdims=True)
    h2_ref[...] = (xn * lax.rsqrt(ms + EPS) * n2_ref[...]).astype(BF16)


def _merge(x2d, ya, yb, yc, proj, wb, wo, n2, tm):
    T, D = x2d.shape
    W = ya.shape[1]
    yspec = pl.BlockSpec((tm, W), lambda i: (i, 0))
    return pl.pallas_call(
        _merge_kernel,
        grid=(T // tm,),
        in_specs=[
            pl.BlockSpec((tm, D), lambda i: (i, 0)),
            yspec, yspec, yspec,
            pl.BlockSpec((tm, D), lambda i: (i, 0)),
            pl.BlockSpec((tm, D), lambda i: (i, 1)),
            pl.BlockSpec((tm, D), lambda i: (i, 2)),
            pl.BlockSpec((N_BRANCH, W, D), lambda i: (0, 0, 0)),
            pl.BlockSpec((D, D), lambda i: (0, 0)),
            pl.BlockSpec((1, D), lambda i: (0, 0)),
        ],
        out_specs=[pl.BlockSpec((tm, D), lambda i: (i, 0)),
                   pl.BlockSpec((tm, D), lambda i: (i, 0))],
        out_shape=[jax.ShapeDtypeStruct((T, D), F32), jax.ShapeDtypeStruct((T, D), BF16)],
        compiler_params=_cparams(("parallel",)),
        name="gated_merge",
    )(x2d, ya, yb, yc, proj, proj, proj, wb, wo, n2)


def _candidate_pairs():
    return [(a, b) for a in range(PEER_TOPK) for b in range(PEER_TOPK)
            if (a + 1) * (b + 1) <= PEER_TOPK]


def _peer_route_kernel(h2_ref, wqt_ref, sk_ref, c1_ref, r2_ref, e1_ref, e2_ref,
                       s_ref, top_ref):
    qT = lax.dot_general(wqt_ref[...], h2_ref[...], _NT,
                         preferred_element_type=F32).astype(BF16)
    kio = lax.broadcasted_iota(jnp.int32, (N_KEYS, LANES), 0).astype(F32)
    pairs = _candidate_pairs()

    ranks = {}
    for h in range(PEER_HEADS):
        for p in range(2):
            r0 = (h * 2 + p) * PEER_HALF
            x0 = jnp.dot(sk_ref[p, h], qT[r0:r0 + PEER_HALF, :],
                         preferred_element_type=F32)
            s_ref[p, h] = x0

            def it(r, xc, p=p, h=h):
                x, rank = xc
                m = jnp.max(x, axis=0, keepdims=True)
                idx = jnp.where(x == m, kio, float(N_KEYS))
                first = jnp.min(idx, axis=0, keepdims=True)
                hit = kio == first
                top_ref[p, r, pl.ds(h, 1), :] = m
                return (jnp.where(hit, -jnp.inf, x),
                        jnp.where(hit, r.astype(F32), rank))

            _, rank = lax.fori_loop(0, PEER_TOPK, it,
                                    (x0, jnp.full((N_KEYS, LANES), float(PEER_TOPK), F32)))
            if p == 0:
                ranks[h] = rank
            else:
                r2_ref[h] = rank.astype(BF16)

    a = [top_ref[0, r] for r in range(PEER_TOPK)]
    b = [top_ref[1, r] for r in range(PEER_TOPK)]
    cand = [a[ia] + b[ib] for (ia, ib) in pairs]
    m0 = cand[0]
    cnt = [jnp.zeros_like(m0) for _ in range(PEER_TOPK)]
    z = jnp.zeros_like(m0)
    for _ in range(PEER_TOPK):
        m = functools.reduce(jnp.maximum, cand)
        z = z + jnp.exp(m - m0)
        found = jnp.zeros(m0.shape, jnp.bool_)
        for ci, (ia, ib) in enumerate(pairs):
            hit = (cand[ci] == m) & jnp.logical_not(found)
            found = found | hit
            cand[ci] = jnp.where(hit, -jnp.inf, cand[ci])
            cnt[ia] = cnt[ia] + jnp.where(hit, 1.0, 0.0)
    inv_z = 1.0 / z

    for h in range(PEER_HEADS):
        c1 = jnp.zeros((N_KEYS, LANES), F32)
        for r in range(PEER_TOPK):
            c1 = jnp.where(ranks[h] == float(r), cnt[r][h:h + 1, :], c1)
        c1_ref[h] = c1
        e1_ref[h] = jnp.exp(s_ref[0, h] - a[0][h:h + 1, :])
        e2_ref[h] = (jnp.exp(s_ref[1, h] - b[0][h:h + 1, :]) * inv_z[h:h + 1, :]).astype(BF16)


def _peer_route(h2, wqt, sk):
    T, D = h2.shape
    oshape_w = jax.ShapeDtypeStruct((PEER_HEADS, N_KEYS, T), F32)
    oshape_bf = jax.ShapeDtypeStruct((PEER_HEADS, N_KEYS, T), BF16)
    ospec = pl.BlockSpec((PEER_HEADS, N_KEYS, LANES), lambda i: (0, 0, i))
    return pl.pallas_call(
        _peer_route_kernel,
        grid=(T // LANES,),
        in_specs=[
            pl.BlockSpec((LANES, D), lambda i: (i, 0)),
            pl.BlockSpec(wqt.shape, lambda i: (0, 0)),
            pl.BlockSpec(sk.shape, lambda i: (0, 0, 0, 0)),
        ],
        out_specs=[ospec, ospec, ospec, ospec],
        out_shape=[oshape_w, oshape_bf, oshape_w, oshape_bf],
        scratch_shapes=[pltpu.VMEM((2, PEER_HEADS, N_KEYS, LANES), F32),
                        pltpu.VMEM((2, PEER_TOPK, PEER_HEADS, LANES), F32)],
        compiler_params=_cparams(("parallel",)),
        name="peer_route",
    )(h2, wqt, sk)


def _peer_dense_kernel(x_ref, h2_ref, u_ref, vt_ref, c1_ref, r2_ref, e1_ref, e2_ref,
                       o_ref, acc_ref, g_ref, cw_ref, ew_ref, *, tm, ce):
    c = pl.program_id(1)
    cur = lax.rem(c, 2)

    @pl.when(c == 0)
    def _():
        acc_ref[...] = jnp.zeros(acc_ref.shape, F32)
        g_ref[1] = jnp.zeros(g_ref.shape[1:], BF16)

    acc_ref[...] += jnp.dot(vt_ref[...], g_ref[1 - cur], preferred_element_type=F32)

    hT = lax.dot_general(u_ref[...], h2_ref[...], _NT, preferred_element_type=F32)
    pack = 16
    span = 4 * pack
    zero = jnp.zeros((pack, tm), BF16)

    nt = tm // LANES
    n_i1 = ce // N_KEYS
    for h in range(PEER_HEADS):
        for t in range(nt):
            slab = pl.ds((h * nt + t) * n_i1, n_i1)
            cw_ref[slab, :] = c1_ref[h, :, t * LANES:(t + 1) * LANES]
            ew_ref[slab, :] = e1_ref[h, :, t * LANES:(t + 1) * LANES]

    def bcast_bf16(ref, h, row):
        tiles = []
        for t in range(nt):
            x = ref[pl.ds((h * nt + t) * n_i1 + row, pack // 2, stride=0), :]
            tiles.append(jnp.concatenate([x, x], axis=0).astype(BF16))
        return jnp.concatenate(tiles, axis=1)

    for ii in range(n_i1):
        for r0 in range(0, N_KEYS, span):
            w = [zero for _ in range(span // pack)]
            for h in range(PEER_HEADS):
                c1b = bcast_bf16(cw_ref, h, ii)
                e1b = bcast_bf16(ew_ref, h, ii)
                for g in range(span // pack):
                    rows = slice(r0 + g * pack, r0 + (g + 1) * pack)
                    w[g] = w[g] + jnp.where(r2_ref[h, rows, :] < c1b,
                                            e1b * e2_ref[h, rows, :], zero)
            e0 = ii * N_KEYS + r0
            hh = hT[e0:e0 + span, :]
            act = (0.5 * hh * (1.0 + lax.erf(hh * (1.0 / math.sqrt(2.0))))).astype(BF16)
            g_ref[cur, e0:e0 + span, :] = jnp.concatenate(w, axis=0) * act

    @pl.when(c == pl.num_programs(1) - 1)
    def _():
        o_ref[...] = x_ref[...] + acc_ref[...].T


def _peer_dense(x2d, h2, u, vt, c1, r2, e1, e2, tm, ce):
    T, D = x2d.shape
    ne = u.shape[0]
    nchunk = ne // ce
    build = lambda c: jnp.minimum(c, nchunk - 1)
    drain = lambda c: jnp.maximum(c - 1, 0)
    rspec = pl.BlockSpec((PEER_HEADS, N_KEYS, tm), lambda i, c: (0, 0, i))
    cspec = pl.BlockSpec((PEER_HEADS, ce // N_KEYS, tm), lambda i, c: (0, build(c), i))
    return pl.pallas_call(
        functools.partial(_peer_dense_kernel, tm=tm, ce=ce),
        grid=(T // tm, nchunk + 1),
        in_specs=[
            pl.BlockSpec((tm, D), lambda i, c: (i, 0)),
            pl.BlockSpec((tm, D), lambda i, c: (i, 0)),
            pl.BlockSpec((ce, D), lambda i, c: (build(c), 0)),
            pl.BlockSpec((D, ce), lambda i, c: (0, drain(c))),
            cspec, rspec, cspec, rspec,
        ],
        out_specs=pl.BlockSpec((tm, D), lambda i, c: (i, 0)),
        out_shape=jax.ShapeDtypeStruct((T, D), F32),
        scratch_shapes=[pltpu.VMEM((D, tm), F32), pltpu.VMEM((2, ce, tm), BF16),
                        pltpu.VMEM((PEER_HEADS * (tm // LANES) * (ce // N_KEYS), LANES), F32),
                        pltpu.VMEM((PEER_HEADS * (tm // LANES) * (ce // N_KEYS), LANES), F32)],
        compiler_params=_cparams(("parallel", "arbitrary")),
        name="peer_dense",
    )(x2d, h2, u, vt, c1, r2, e1, e2)


def _rope_tables(seq):
    inv = 1.0 / (ROPE_THETA ** (jnp.arange(0, ROPE_DIMS, 2, dtype=F32) / ROPE_DIMS))
    ang = jnp.arange(seq, dtype=F32)[:, None] * inv[None, :]
    cos, sin = jnp.cos(ang), jnp.sin(ang)
    half = ROPE_DIMS // 2
    pad = HEAD_DIM - ROPE_DIMS
    one = jnp.ones((seq, pad), F32)
    zer = jnp.zeros((seq, pad), F32)
    zh = jnp.zeros((seq, half), F32)
    rc = jnp.concatenate([cos, cos, one], axis=1)
    rs1 = jnp.concatenate([-sin, zh, zer], axis=1)
    rs2 = jnp.concatenate([zh, sin, zer], axis=1)
    rep = LANES // HEAD_DIM
    return jnp.tile(rc, (1, rep)), jnp.tile(rs1, (1, rep)), jnp.tile(rs2, (1, rep))


def _band_bias_mask(rel_bias, tq):
    nwin = B_LEFT_CHUNKS * CHUNK // tq + 1
    r = np.arange(tq)[:, None]
    c = np.arange(nwin * tq)[None, :]
    kchunk = c // CHUNK - (nwin - 1) * (tq // CHUNK)
    qchunk = r // CHUNK
    valid = (kchunk <= qchunk) & (kchunk >= qchunk - B_LEFT_CHUNKS)
    wwin, d0 = nwin * tq, (nwin - 1) * tq
    period = wwin + tq
    u = np.arange(-(tq - 1), wwin)
    perm = np.zeros(period, np.int32)
    perm[u % period] = np.clip(d0 - u, -REL_CLIP, REL_CLIP) + REL_CLIP
    line = rel_bias.astype(F32)[:, perm]
    nh = rel_bias.shape[0]
    skew = jnp.broadcast_to(line[:, None, :], (nh, tq, period)).reshape(nh, tq * period)
    bias = skew[:, :tq * (period - 1)].reshape(nh, tq, period - 1)[:, :, :wwin]
    return jnp.where(jnp.asarray(valid)[None], bias * LOG2E, NEG_BIG)


def _layer(x2d, p, layer, batch, seq, consts):
    T, D = x2d.shape
    lam_init = 0.8 - 0.6 * math.exp(-0.3 * layer)
    tq = consts["tq"]

    w_in = p["w_in"]
    n_qkv = w_in.shape[1] - GATE_COLS
    w_perm = jnp.concatenate([w_in[:, n_qkv:], w_in[:, :n_qkv]], axis=1).astype(BF16)
    tile_g = lambda g: jnp.tile(g.astype(F32), COL_TILE // HEAD_DIM)
    qkg = jnp.stack([tile_g(p["a_qnorm_g"]), tile_g(p["a_knorm_g"]),
                     tile_g(p["b_qnorm_g"]), tile_g(p["b_knorm_g"])])
    proj = _in_projection(x2d, p["norm1_g"].reshape(1, D).astype(F32), w_perm,
                          consts["rc"], consts["rs1"], consts["rs2"], qkg, consts["bd"],
                          seq, consts["tm_in"])

    lamv = jnp.stack([p["lam_q1"], p["lam_k1"], p["lam_q2"], p["lam_k2"]]).astype(F32)
    ya = _diff_attention(proj, lamv, p["a_subln_g"].reshape(1, LANES).astype(F32),
                         batch, seq, tq, lam_init)
    yb = _band_attention(proj, _band_bias_mask(p["b_rel_bias"], tq), batch, seq, tq)
    yc = _stick_attention(proj, consts["tri"], batch, seq, tq)

    xn, h2 = _merge(x2d, ya, yb, yc, proj, p["w_branch"].astype(BF16),
                    p["w_out"].astype(BF16), p["norm2_g"].reshape(1, D).astype(F32),
                    consts["tm_merge"])

    wqt = p["peer_wq"].T.astype(BF16)
    sk = p["peer_subkeys"].astype(BF16)
    c1, r2, e1, e2 = _peer_route(h2, wqt, sk)
    u = p["peer_u"].astype(BF16)
    vt = p["peer_v"].T.astype(BF16)
    return _peer_dense(xn, h2, u, vt, c1, r2, e1, e2, consts["tm_peer"], consts["ce"])


def kernel(x, norm1_g, w_in, a_qnorm_g, a_knorm_g, lam_q1, lam_k1, lam_q2, lam_k2, a_subln_g, b_qnorm_g, b_knorm_g, b_rel_bias, w_branch, w_out, norm2_g, peer_wq, peer_subkeys, peer_u, peer_v):
    B, S, D = x.shape
    T = B * S
    depth = w_in.shape[0]
    tq = min(256, S)
    rc, rs1, rs2 = _rope_tables(S)
    seg = np.arange(COL_TILE) // HEAD_DIM
    bd = jnp.asarray(seg[:, None] == seg[None, :], BF16)
    kk = np.arange(tq)
    tri = jnp.asarray(kk[:, None] > kk[None, :], BF16)
    consts = dict(tq=tq, rc=rc, rs1=rs1, rs2=rs2, bd=bd, tri=tri,
                  tm_in=min(1024, S), tm_merge=min(512, T), tm_peer=min(512, T), ce=1024)
    params = dict(norm1_g=norm1_g, w_in=w_in, a_qnorm_g=a_qnorm_g, a_knorm_g=a_knorm_g,
                  lam_q1=lam_q1, lam_k1=lam_k1, lam_q2=lam_q2, lam_k2=lam_k2,
                  a_subln_g=a_subln_g, b_qnorm_g=b_qnorm_g, b_knorm_g=b_knorm_g,
                  b_rel_bias=b_rel_bias, w_branch=w_branch, w_out=w_out, norm2_g=norm2_g,
                  peer_wq=peer_wq, peer_subkeys=peer_subkeys, peer_u=peer_u, peer_v=peer_v)
    x2d = x.reshape(T, D)
    for layer in range(depth):
        x2d = _layer(x2d, {k: v[layer] for k, v in params.items()}, layer, B, S, consts)
    return x2d.reshape(B, S, D)
```

```python
import functools
import math

import numpy as np
import jax
import jax.numpy as jnp
from jax import lax
from jax.experimental import pallas as pl
from jax.experimental.pallas import tpu as pltpu

F32 = jnp.float32
BF16 = jnp.bfloat16

CHUNK = 64
HEAD_DIM = 64
EPS = 1e-6
ROPE_THETA = 500000.0
ROPE_DIMS = HEAD_DIM // 4
A_HEADS = 4
B_HEADS = 8
B_LEFT_CHUNKS = 8
REL_CLIP = 128
C_HEADS = 8
N_BRANCH = 3
PEER_HEADS = 8
N_KEYS = 128
PEER_HALF = 128
PEER_TOPK = 16

LANES = 128
COL_TILE = 512
NEG_BIG = -1e30
LOG2E = math.log2(math.e)
VMEM_LIMIT = 56 * 1024 * 1024

GATE_COLS = 3 * 1024
_BLK = lambda tile: (GATE_COLS + tile * COL_TILE) // LANES
AQ_BLK, AK_BLK, AV_BLK = _BLK(0), _BLK(1), _BLK(2)
BQ_BLK, BK_BLK, BV_BLK = _BLK(3), _BLK(4), _BLK(5)
CQ_BLK, CK_BLK, CV_BLK = _BLK(6), _BLK(7), _BLK(8)

_NT = (((1,), (1,)), ((), ()))


def _cparams(sem, flags=None):
    return pltpu.CompilerParams(dimension_semantics=sem, vmem_limit_bytes=VMEM_LIMIT,
                                flags=flags)


def _inproj_kernel(x_ref, g_ref, w_ref, rc_ref, rs1_ref, rs2_ref, qkg_ref, bd_ref,
                   o_ref, hn_ref):
    j = pl.program_id(1)
    n_gate = GATE_COLS // COL_TILE

    @pl.when(j == 0)
    def _():
        x = x_ref[...]
        ms = jnp.mean(x * x, axis=-1, keepdims=True)
        hn_ref[...] = (x * lax.rsqrt(ms + EPS) * g_ref[...]).astype(BF16)

    acc = jnp.dot(hn_ref[...], w_ref[...], preferred_element_type=F32)

    def seg_norm(y, row):
        sq = y * y
        hi = sq.astype(BF16)
        lo = (sq - hi.astype(F32)).astype(BF16)
        ss = (jnp.dot(hi, bd_ref[...], preferred_element_type=F32)
              + jnp.dot(lo, bd_ref[...], preferred_element_type=F32))
        return y * lax.rsqrt(ss * (1.0 / HEAD_DIM) + EPS) * qkg_ref[pl.ds(row, 1), :]

    def rope(y):
        outs = []
        for c in range(COL_TILE // LANES):
            ys = y[:, c * LANES:(c + 1) * LANES]
            outs.append(ys * rc_ref[...]
                        + pltpu.roll(ys, LANES - ROPE_DIMS // 2, 1) * rs1_ref[...]
                        + pltpu.roll(ys, ROPE_DIMS // 2, 1) * rs2_ref[...])
        return jnp.concatenate(outs, axis=1)

    scale = HEAD_DIM ** -0.5 * LOG2E

    @pl.when(j < n_gate)
    def _():
        o_ref[...] = jax.nn.sigmoid(acc).astype(BF16)

    @pl.when(j == n_gate + 0)
    def _():
        o_ref[...] = (rope(seg_norm(acc, 0)) * scale).astype(BF16)

    @pl.when(j == n_gate + 1)
    def _():
        o_ref[...] = rope(seg_norm(acc, 1)).astype(BF16)

    @pl.when(j == n_gate + 3)
    def _():
        o_ref[...] = (seg_norm(acc, 2) * scale).astype(BF16)

    @pl.when(j == n_gate + 4)
    def _():
        o_ref[...] = seg_norm(acc, 3).astype(BF16)

    @pl.when(j == n_gate + 6)
    def _():
        o_ref[...] = (acc * scale).astype(BF16)

    plain = ((j == n_gate + 2) | (j == n_gate + 5) | (j == n_gate + 7) | (j == n_gate + 8))

    @pl.when(plain)
    def _():
        o_ref[...] = acc.astype(BF16)


def _in_projection(x2d, g1, w, rc, rs1, rs2, qkg, bd, seq, tm):
    T, D = x2d.shape
    ncol = w.shape[1]
    nseq = seq // tm
    return pl.pallas_call(
        _inproj_kernel,
        grid=(T // tm, ncol // COL_TILE),
        in_specs=[
            pl.BlockSpec((tm, D), lambda i, j: (i, 0)),
            pl.BlockSpec((1, D), lambda i, j: (0, 0)),
            pl.BlockSpec((D, COL_TILE), lambda i, j: (0, j)),
            pl.BlockSpec((tm, LANES), lambda i, j: (i % nseq, 0)),
            pl.BlockSpec((tm, LANES), lambda i, j: (i % nseq, 0)),
            pl.BlockSpec((tm, LANES), lambda i, j: (i % nseq, 0)),
            pl.BlockSpec((4, COL_TILE), lambda i, j: (0, 0)),
            pl.BlockSpec((COL_TILE, COL_TILE), lambda i, j: (0, 0)),
        ],
        out_specs=pl.BlockSpec((tm, COL_TILE), lambda i, j: (i, j)),
        out_shape=jax.ShapeDtypeStruct((T, ncol), BF16),
        scratch_shapes=[pltpu.VMEM((tm, D), BF16)],
        compiler_params=_cparams(("parallel", "arbitrary")),
        name="in_projection",
    )(x2d, g1, w, rc, rs1, rs2, qkg, bd)


def _diff_attn_kernel(lam_ref, sg_ref, q_ref, k_ref, v_ref, o_ref, m_ref, l_ref, acc_ref,
                      *, tq, lam_init):
    i = pl.program_id(2)
    q = q_ref[...]
    lane = lax.broadcasted_iota(jnp.int32, (tq, LANES), 1)
    zero = jnp.zeros_like(q)
    qs = jnp.concatenate([jnp.where(lane < HEAD_DIM, q, zero),
                          jnp.where(lane >= HEAD_DIM, q, zero)], axis=0)

    m_ref[...] = jnp.full(m_ref.shape, NEG_BIG, F32)
    l_ref[...] = jnp.zeros(l_ref.shape, F32)
    acc_ref[...] = jnp.zeros(acc_ref.shape, F32)

    nt = tq // LANES

    def step(blocks, masked):
        ss = []
        for j in blocks:
            k = k_ref[pl.ds(pl.multiple_of(j * tq, tq), tq), :]
            s = lax.dot_general(qs, k, _NT, preferred_element_type=F32)
            if masked:
                row = lax.broadcasted_iota(jnp.int32, (2 * tq, tq), 0)
                col = lax.broadcasted_iota(jnp.int32, (2 * tq, tq), 1)
                qrow = jnp.where(row >= tq, row - tq, row)
                shift = CHUNK.bit_length() - 1
                s = jnp.where(jnp.right_shift(col, shift) <= jnp.right_shift(qrow, shift),
                              s, NEG_BIG)
            ss.append(s)
        m_prev = m_ref[...]
        smax = functools.reduce(jnp.maximum, [s[:, c * LANES:(c + 1) * LANES]
                                              for s in ss for c in range(nt)])
        m_new = jnp.maximum(m_prev, jnp.max(smax, axis=-1, keepdims=True))
        alpha = jnp.exp2(m_prev - m_new)
        acc = alpha * acc_ref[...]
        psum = None
        for j, s in zip(blocks, ss):
            ps = [jnp.exp2(s[:, c * LANES:(c + 1) * LANES] - m_new) for c in range(nt)]
            psum = functools.reduce(jnp.add, ps if psum is None else [psum] + ps)
            v = v_ref[pl.ds(pl.multiple_of(j * tq, tq), tq), :]
            acc = acc + jnp.dot(jnp.concatenate(ps, axis=1).astype(BF16), v,
                                preferred_element_type=F32)
        l_ref[...] = alpha * l_ref[...] + jnp.sum(psum, axis=-1, keepdims=True)
        acc_ref[...] = acc
        m_ref[...] = m_new

    def pair_body(t, c):
        step([2 * t, 2 * t + 1], False)
        return c

    lax.fori_loop(0, i // 2, pair_body, 0)

    @pl.when(i % 2 == 1)
    def _():
        step([i - 1], False)

    step([i], True)

    lam = (jnp.exp(jnp.sum(lam_ref[0:1, :] * lam_ref[1:2, :], axis=-1, keepdims=True))
           - jnp.exp(jnp.sum(lam_ref[2:3, :] * lam_ref[3:4, :], axis=-1, keepdims=True))
           + lam_init)
    o = acc_ref[...] / l_ref[...]
    o = o[:tq] - lam * o[tq:]
    ms = jnp.mean(o * o, axis=-1, keepdims=True)
    o = o * lax.rsqrt(ms + EPS) * sg_ref[...]
    o_ref[...] = (o * (1.0 - lam_init)).astype(BF16)


def _diff_attention(proj, lamv, subln_g, batch, seq, tq, lam_init):
    T = proj.shape[0]
    nq = seq // tq
    return pl.pallas_call(
        functools.partial(_diff_attn_kernel, tq=tq, lam_init=lam_init),
        grid=(batch, A_HEADS, nq),
        in_specs=[
            pl.BlockSpec((4, HEAD_DIM), lambda b, h, i: (0, 0)),
            pl.BlockSpec((1, LANES), lambda b, h, i: (0, 0)),
            pl.BlockSpec((tq, LANES), lambda b, h, i: (b * nq + i, AQ_BLK + h)),
            pl.BlockSpec((seq, LANES), lambda b, h, i: (b, AK_BLK + h)),
            pl.BlockSpec((seq, LANES), lambda b, h, i: (b, AV_BLK + h)),
        ],
        out_specs=pl.BlockSpec((tq, LANES), lambda b, h, i: (b * nq + i, h)),
        out_shape=jax.ShapeDtypeStruct((T, A_HEADS * LANES), BF16),
        scratch_shapes=[pltpu.VMEM((2 * tq, LANES), F32), pltpu.VMEM((2 * tq, LANES), F32),
                        pltpu.VMEM((2 * tq, LANES), F32)],
        compiler_params=_cparams(("parallel", "parallel", "arbitrary")),
        name="diff_attention",
    )(lamv, subln_g, proj, proj, proj)


def _band_attn_kernel(bm_ref, q_ref, k_ref, v_ref, o_ref, *, tq):
    i = pl.program_id(2)
    q = q_ref[...]
    lane = lax.broadcasted_iota(jnp.int32, (tq, LANES), 1)
    zero = jnp.zeros_like(q)
    nwin = B_LEFT_CHUNKS * CHUNK // tq + 1
    outs = []
    for half in range(2):
        sel = (lane < HEAD_DIM) if half == 0 else (lane >= HEAD_DIM)
        qh = jnp.where(sel, q, zero)
        ss = []
        for blk in range(nwin):
            kb = i - (nwin - 1) + blk
            kbc = jnp.maximum(kb, 0)
            k = k_ref[pl.ds(pl.multiple_of(kbc * tq, tq), tq), :]
            s = lax.dot_general(qh, k, _NT, preferred_element_type=F32)
            s = s + bm_ref[half, :, blk * tq:(blk + 1) * tq]
            ss.append(jnp.where(kb >= 0, s, NEG_BIG))
        s = jnp.concatenate(ss, axis=1)
        m = jnp.max(s, axis=-1, keepdims=True)
        p = jnp.exp2(s - m)
        l = jnp.sum(p, axis=-1, keepdims=True)
        pb = p.astype(BF16)
        o = jnp.zeros((tq, LANES), F32)
        for blk in range(nwin):
            kbc = jnp.maximum(i - (nwin - 1) + blk, 0)
            v = v_ref[pl.ds(pl.multiple_of(kbc * tq, tq), tq), :]
            o = o + jnp.dot(pb[:, blk * tq:(blk + 1) * tq], v, preferred_element_type=F32)
        outs.append(o / l)
    o_ref[...] = jnp.where(lane < HEAD_DIM, outs[0], outs[1]).astype(BF16)


def _band_attention(proj, biasmask, batch, seq, tq):
    T = proj.shape[0]
    nq = seq // tq
    npair = B_HEADS // 2
    wwin = biasmask.shape[-1]
    return pl.pallas_call(
        functools.partial(_band_attn_kernel, tq=tq),
        grid=(batch, npair, nq),
        in_specs=[
            pl.BlockSpec((2, tq, wwin), lambda b, h, i: (h, 0, 0)),
            pl.BlockSpec((tq, LANES), lambda b, h, i: (b * nq + i, BQ_BLK + h)),
            pl.BlockSpec((seq, LANES), lambda b, h, i: (b, BK_BLK + h)),
            pl.BlockSpec((seq, LANES), lambda b, h, i: (b, BV_BLK + h)),
        ],
        out_specs=pl.BlockSpec((tq, LANES), lambda b, h, i: (b * nq + i, h)),
        out_shape=jax.ShapeDtypeStruct((T, npair * LANES), BF16),
        compiler_params=_cparams(("parallel", "parallel", "arbitrary")),
        name="band_attention",
    )(biasmask, proj, proj, proj)


def _stick_attn_kernel(tri_ref, q_ref, k_ref, v_ref, o_ref, carry_ref, acc_ref, *, tq):
    i = pl.program_id(2)
    q = q_ref[...]
    lane = lax.broadcasted_iota(jnp.int32, (tq, LANES), 1)
    zero = jnp.zeros_like(q)
    qh = (jnp.where(lane < HEAD_DIM, q, zero), jnp.where(lane >= HEAD_DIM, q, zero))

    carry_ref[...] = jnp.zeros(carry_ref.shape, F32)
    acc_ref[...] = jnp.zeros(acc_ref.shape, F32)

    nt = tq // LANES
    lane_tiles = lambda t: [t[:, c * LANES:(c + 1) * LANES] for c in range(nt)]

    def step(blocks, masked):
        if masked:
            row = lax.broadcasted_iota(jnp.int32, (tq, tq), 0)
            col = lax.broadcasted_iota(jnp.int32, (tq, tq), 1)
            strict = col < row
        for half in range(2):
            parts = []
            for j in blocks:
                k = k_ref[pl.ds(pl.multiple_of(j * tq, tq), tq), :]
                z2 = lax.dot_general(qh[half], k, _NT, preferred_element_type=F32)
                nz2 = -z2
                lg2 = jnp.log(1.0 + jnp.exp2(jnp.minimum(z2, nz2))) * LOG2E
                lk = jnp.minimum(nz2, 0.0) - lg2
                ls = lk + z2
                if masked:
                    lk = jnp.where(strict, lk, 0.0)
                later = jnp.dot(lk.astype(BF16), tri_ref[...], preferred_element_type=F32)
                tot = jnp.sum(functools.reduce(jnp.add, lane_tiles(lk)),
                              axis=-1, keepdims=True)
                parts.append((j, ls, later, tot))
            carry = carry_ref[half]
            acc = acc_ref[half]
            for j, ls, later, tot in parts:
                a = jnp.concatenate([jnp.exp2(x + (y + carry))
                                     for x, y in zip(lane_tiles(ls), lane_tiles(later))], axis=1)
                if masked:
                    a = jnp.where(strict, a, 0.0)
                v = v_ref[pl.ds(pl.multiple_of(j * tq, tq), tq), :]
                acc = acc + jnp.dot(a.astype(BF16), v, preferred_element_type=F32)
                carry = carry + tot
            acc_ref[half] = acc
            carry_ref[half] = carry

    step([i], True)

    def pair_body(t, c):
        j = i - 1 - 2 * t
        step([j, j - 1], False)
        return c

    lax.fori_loop(0, i // 2, pair_body, 0)

    @pl.when(i % 2 == 1)
    def _():
        step([0], False)
    o_ref[...] = jnp.where(lane < HEAD_DIM, acc_ref[0], acc_ref[1]).astype(BF16)


def _stick_attention(proj, tri, batch, seq, tq):
    T = proj.shape[0]
    nq = seq // tq
    npair = C_HEADS // 2
    return pl.pallas_call(
        functools.partial(_stick_attn_kernel, tq=tq),
        grid=(batch, npair, nq),
        in_specs=[
            pl.BlockSpec((tq, tq), lambda b, h, i: (0, 0)),
            pl.BlockSpec((tq, LANES), lambda b, h, i: (b * nq + i, CQ_BLK + h)),
            pl.BlockSpec((seq, LANES), lambda b, h, i: (b, CK_BLK + h)),
            pl.BlockSpec((seq, LANES), lambda b, h, i: (b, CV_BLK + h)),
        ],
        out_specs=pl.BlockSpec((tq, LANES), lambda b, h, i: (b * nq + i, h)),
        out_shape=jax.ShapeDtypeStruct((T, npair * LANES), BF16),
        scratch_shapes=[pltpu.VMEM((2, tq, LANES), F32), pltpu.VMEM((2, tq, LANES), F32)],
        compiler_params=_cparams(("parallel", "parallel", "arbitrary")),
        name="stick_attention",
    )(tri, proj, proj, proj)


def _merge_kernel(x_ref, ya_ref, yb_ref, yc_ref, g0_ref, g1_ref, g2_ref, wb_ref, wo_ref,
                  n2_ref, xo_ref, h2_ref):
    merged = None
    for n, (y_ref, g_ref) in enumerate(((ya_ref, g0_ref), (yb_ref, g1_ref), (yc_ref, g2_ref))):
        up = jnp.dot(y_ref[...], wb_ref[n], preferred_element_type=F32)
        term = g_ref[...].astype(F32) * up
        merged = term if merged is None else merged + term
    xn = x_ref[...] + jnp.dot(merged.astype(BF16), wo_ref[...], preferred_element_type=F32)
    xo_ref[...] = xn
    ms = jnp.mean(xn * xn, axis=-1, keepdims=True)
    h2_ref[...] = (xn * lax.rsqrt(ms + EPS) * n2_ref[...]).astype(BF16)


def _merge(x2d, ya, yb, yc, proj, wb, wo, n2, tm):
    T, D = x2d.shape
    W = ya.shape[1]
    yspec = pl.BlockSpec((tm, W), lambda i: (i, 0))
    return pl.pallas_call(
        _merge_kernel,
        grid=(T // tm,),
        in_specs=[
            pl.BlockSpec((tm, D), lambda i: (i, 0)),
            yspec, yspec, yspec,
            pl.BlockSpec((tm, D), lambda i: (i, 0)),
            pl.BlockSpec((tm, D), lambda i: (i, 1)),
            pl.BlockSpec((tm, D), lambda i: (i, 2)),
            pl.BlockSpec((N_BRANCH, W, D), lambda i: (0, 0, 0)),
            pl.BlockSpec((D, D), lambda i: (0, 0)),
            pl.BlockSpec((1, D), lambda i: (0, 0)),
        ],
        out_specs=[pl.BlockSpec((tm, D), lambda i: (i, 0)),
                   pl.BlockSpec((tm, D), lambda i: (i, 0))],
        out_shape=[jax.ShapeDtypeStruct((T, D), F32), jax.ShapeDtypeStruct((T, D), BF16)],
        compiler_params=_cparams(("parallel",)),
        name="gated_merge",
    )(x2d, ya, yb, yc, proj, proj, proj, wb, wo, n2)


def _candidate_pairs():
    return [(a, b) for a in range(PEER_TOPK) for b in range(PEER_TOPK)
            if (a + 1) * (b + 1) <= PEER_TOPK]


def _peer_route_kernel(h2_ref, wqt_ref, sk_ref, c1_ref, r2_ref, e1_ref, e2_ref,
                       s_ref, xw_ref, rk_ref, top_ref):
    qT = lax.dot_general(wqt_ref[...], h2_ref[...], _NT,
                         preferred_element_type=F32).astype(BF16)
    kio = lax.broadcasted_iota(jnp.int32, (N_KEYS, LANES), 0).astype(F32)
    pairs = _candidate_pairs()

    group = 2
    for h0 in range(0, PEER_HEADS, group):
        probs = [(h, p) for h in range(h0, h0 + group) for p in range(2)]
        for h, p in probs:
            r0 = (h * 2 + p) * PEER_HALF
            x0 = jnp.dot(sk_ref[p, h], qT[r0:r0 + PEER_HALF, :],
                         preferred_element_type=F32)
            s_ref[p, h] = x0
            xw_ref[p, h] = x0
            rk_ref[p, h] = jnp.full((N_KEYS, LANES), float(PEER_TOPK), F32)

        def it(r, c, probs=probs):
            rf = jnp.asarray(r, F32)
            for h, p in probs:
                x = xw_ref[p, h]
                m = jnp.max(x, axis=0, keepdims=True)
                idx = jnp.where(x == m, kio, float(N_KEYS))
                first = jnp.min(idx, axis=0, keepdims=True)
                hit = kio == first
                top_ref[p, r, pl.ds(h, 1), :] = m
                xw_ref[p, h] = jnp.where(hit, -jnp.inf, x)
                rk_ref[p, h] = jnp.where(hit, rf, rk_ref[p, h])
            return c

        lax.fori_loop(0, PEER_TOPK, it, 0)
    ranks = {h: rk_ref[0, h] for h in range(PEER_HEADS)}
    for h in range(PEER_HEADS):
        r2_ref[h] = rk_ref[1, h].astype(BF16)

    a = [top_ref[0, r] for r in range(PEER_TOPK)]
    b = [top_ref[1, r] for r in range(PEER_TOPK)]
    cand = [a[ia] + b[ib] for (ia, ib) in pairs]
    m0 = cand[0]
    cnt = [jnp.zeros_like(m0) for _ in range(PEER_TOPK)]
    z = jnp.zeros_like(m0)
    for _ in range(PEER_TOPK):
        m = functools.reduce(jnp.maximum, cand)
        z = z + jnp.exp(m - m0)
        found = jnp.zeros(m0.shape, jnp.bool_)
        for ci, (ia, ib) in enumerate(pairs):
            hit = (cand[ci] == m) & jnp.logical_not(found)
            found = found | hit
            cand[ci] = jnp.where(hit, -jnp.inf, cand[ci])
            cnt[ia] = cnt[ia] + jnp.where(hit, 1.0, 0.0)
    inv_z = 1.0 / z

    for h in range(PEER_HEADS):
        c1 = jnp.zeros((N_KEYS, LANES), F32)
        for r in range(PEER_TOPK):
            c1 = jnp.where(ranks[h] == float(r), cnt[r][h:h + 1, :], c1)
        c1_ref[h] = c1
        e1_ref[h] = jnp.exp(s_ref[0, h] - a[0][h:h + 1, :])
        e2_ref[h] = (jnp.exp(s_ref[1, h] - b[0][h:h + 1, :]) * inv_z[h:h + 1, :]).astype(BF16)


def _peer_route(h2, wqt, sk):
    T, D = h2.shape
    oshape_w = jax.ShapeDtypeStruct((PEER_HEADS, N_KEYS, T), F32)
    oshape_bf = jax.ShapeDtypeStruct((PEER_HEADS, N_KEYS, T), BF16)
    ospec = pl.BlockSpec((PEER_HEADS, N_KEYS, LANES), lambda i: (0, 0, i))
    return pl.pallas_call(
        _peer_route_kernel,
        grid=(T // LANES,),
        in_specs=[
            pl.BlockSpec((LANES, D), lambda i: (i, 0)),
            pl.BlockSpec(wqt.shape, lambda i: (0, 0)),
            pl.BlockSpec(sk.shape, lambda i: (0, 0, 0, 0)),
        ],
        out_specs=[ospec, ospec, ospec, ospec],
        out_shape=[oshape_w, oshape_bf, oshape_w, oshape_bf],
        scratch_shapes=[pltpu.VMEM((2, PEER_HEADS, N_KEYS, LANES), F32),
                        pltpu.VMEM((2, PEER_HEADS, N_KEYS, LANES), F32),
                        pltpu.VMEM((2, PEER_HEADS, N_KEYS, LANES), F32),
                        pltpu.VMEM((2, PEER_TOPK, PEER_HEADS, LANES), F32)],
        compiler_params=_cparams(("parallel",)),
        name="peer_route",
    )(h2, wqt, sk)


def _peer_dense_kernel(x_ref, h2_ref, u_ref, vt_ref, c1_ref, r2_ref, e1_ref, e2_ref,
                       o_ref, acc_ref, ga_ref, gb_ref, w_ref, cw_ref, ew_ref, *, tm, ce):
    c = pl.program_id(1)
    pack = 16
    span = 4 * pack
    nt = tm // LANES
    n_i1 = ce // N_KEYS

    @pl.when(c == 0)
    def _():
        acc_ref[...] = jnp.zeros(acc_ref.shape, F32)
        gb_ref[...] = jnp.zeros(gb_ref.shape, BF16)

    def bcast_bf16(ref, h, row):
        tiles = []
        for t in range(nt):
            x = ref[pl.ds((h * nt + t) * n_i1 + row, pack // 2, stride=0), :]
            tiles.append(jnp.concatenate([x, x], axis=0).astype(BF16))
        return jnp.concatenate(tiles, axis=1)

    def step(g_cur, g_prev):
        for h in range(PEER_HEADS):
            for t in range(nt):
                slab = pl.ds((h * nt + t) * n_i1, n_i1)
                cw_ref[slab, :] = c1_ref[h, :, t * LANES:(t + 1) * LANES]
                ew_ref[slab, :] = e1_ref[h, :, t * LANES:(t + 1) * LANES]

        zero = jnp.zeros((pack, tm), BF16)
        for ii in range(n_i1):
            for r0 in range(0, N_KEYS, span):
                w = [zero for _ in range(span // pack)]
                for h in range(PEER_HEADS):
                    c1b = bcast_bf16(cw_ref, h, ii)
                    e1b = bcast_bf16(ew_ref, h, ii)
                    for g in range(span // pack):
                        rows = slice(r0 + g * pack, r0 + (g + 1) * pack)
                        w[g] = w[g] + e1b * jnp.where(r2_ref[h, rows, :] < c1b,
                                                      e2_ref[h, rows, :], zero)
                e0 = ii * N_KEYS + r0
                w_ref[e0:e0 + span, :] = jnp.concatenate(w, axis=0)

        hT = lax.dot_general(u_ref[...], h2_ref[...], _NT,
                             preferred_element_type=F32)
        for e0 in range(0, ce, span):
            hh = hT[e0:e0 + span, :]
            act = (0.5 * hh * (1.0 + lax.erf(hh * (1.0 / math.sqrt(2.0))))).astype(BF16)
            g_cur[e0:e0 + span, :] = w_ref[e0:e0 + span, :] * act

        acc_ref[...] += jnp.dot(vt_ref[...], g_prev[...], preferred_element_type=F32)

    @pl.when(lax.rem(c, 2) == 0)
    def _():
        step(ga_ref, gb_ref)

    @pl.when(lax.rem(c, 2) == 1)
    def _():
        step(gb_ref, ga_ref)

    @pl.when(c == pl.num_programs(1) - 1)
    def _():
        o_ref[...] = x_ref[...] + acc_ref[...].T


def _peer_dense(x2d, h2, u, vt, c1, r2, e1, e2, tm, ce):
    T, D = x2d.shape
    ne = u.shape[0]
    nchunk = ne // ce
    build = lambda c: jnp.minimum(c, nchunk - 1)
    drain = lambda c: jnp.maximum(c - 1, 0)
    rspec = pl.BlockSpec((PEER_HEADS, N_KEYS, tm), lambda i, c: (0, 0, i))
    cspec = pl.BlockSpec((PEER_HEADS, ce // N_KEYS, tm), lambda i, c: (0, build(c), i))
    return pl.pallas_call(
        functools.partial(_peer_dense_kernel, tm=tm, ce=ce),
        grid=(T // tm, nchunk + 1),
        in_specs=[
            pl.BlockSpec((tm, D), lambda i, c: (i, 0)),
            pl.BlockSpec((tm, D), lambda i, c: (i, 0)),
            pl.BlockSpec((ce, D), lambda i, c: (build(c), 0)),
            pl.BlockSpec((D, ce), lambda i, c: (0, drain(c))),
            cspec, rspec, cspec, rspec,
        ],
        out_specs=pl.BlockSpec((tm, D), lambda i, c: (i, 0)),
        out_shape=jax.ShapeDtypeStruct((T, D), F32),
        scratch_shapes=[pltpu.VMEM((D, tm), F32),
                        pltpu.VMEM((ce, tm), BF16), pltpu.VMEM((ce, tm), BF16),
                        pltpu.VMEM((ce, tm), BF16),
                        pltpu.VMEM((PEER_HEADS * (tm // LANES) * (ce // N_KEYS), LANES), F32),
                        pltpu.VMEM((PEER_HEADS * (tm // LANES) * (ce // N_KEYS), LANES), F32)],
        compiler_params=_cparams(("parallel", "arbitrary")),
        name="peer_dense",
    )(x2d, h2, u, vt, c1, r2, e1, e2)


def _rope_tables(seq):
    inv = 1.0 / (ROPE_THETA ** (jnp.arange(0, ROPE_DIMS, 2, dtype=F32) / ROPE_DIMS))
    ang = jnp.arange(seq, dtype=F32)[:, None] * inv[None, :]
    cos, sin = jnp.cos(ang), jnp.sin(ang)
    half = ROPE_DIMS // 2
    pad = HEAD_DIM - ROPE_DIMS
    one = jnp.ones((seq, pad), F32)
    zer = jnp.zeros((seq, pad), F32)
    zh = jnp.zeros((seq, half), F32)
    rc = jnp.concatenate([cos, cos, one], axis=1)
    rs1 = jnp.concatenate([-sin, zh, zer], axis=1)
    rs2 = jnp.concatenate([zh, sin, zer], axis=1)
    rep = LANES // HEAD_DIM
    return jnp.tile(rc, (1, rep)), jnp.tile(rs1, (1, rep)), jnp.tile(rs2, (1, rep))


def _band_bias_mask(rel_bias, tq):
    nwin = B_LEFT_CHUNKS * CHUNK // tq + 1
    r = np.arange(tq)[:, None]
    c = np.arange(nwin * tq)[None, :]
    kchunk = c // CHUNK - (nwin - 1) * (tq // CHUNK)
    qchunk = r // CHUNK
    valid = (kchunk <= qchunk) & (kchunk >= qchunk - B_LEFT_CHUNKS)
    wwin, d0 = nwin * tq, (nwin - 1) * tq
    period = wwin + tq
    u = np.arange(-(tq - 1), wwin)
    perm = np.zeros(period, np.int32)
    perm[u % period] = np.clip(d0 - u, -REL_CLIP, REL_CLIP) + REL_CLIP
    line = rel_bias.astype(F32)[:, perm]
    nh = rel_bias.shape[0]
    skew = jnp.broadcast_to(line[:, None, :], (nh, tq, period)).reshape(nh, tq * period)
    bias = skew[:, :tq * (period - 1)].reshape(nh, tq, period - 1)[:, :, :wwin]
    return jnp.where(jnp.asarray(valid)[None], bias * LOG2E, NEG_BIG)


def _layer(x2d, p, layer, batch, seq, consts):
    T, D = x2d.shape
    lam_init = 0.8 - 0.6 * math.exp(-0.3 * layer)
    tq = consts["tq"]

    w_in = p["w_in"]
    n_qkv = w_in.shape[1] - GATE_COLS
    w_perm = jnp.concatenate([w_in[:, n_qkv:], w_in[:, :n_qkv]], axis=1).astype(BF16)
    tile_g = lambda g: jnp.tile(g.astype(F32), COL_TILE // HEAD_DIM)
    qkg = jnp.stack([tile_g(p["a_qnorm_g"]), tile_g(p["a_knorm_g"]),
                     tile_g(p["b_qnorm_g"]), tile_g(p["b_knorm_g"])])
    proj = _in_projection(x2d, p["norm1_g"].reshape(1, D).astype(F32), w_perm,
                          consts["rc"], consts["rs1"], consts["rs2"], qkg, consts["bd"],
                          seq, consts["tm_in"])

    lamv = jnp.stack([p["lam_q1"], p["lam_k1"], p["lam_q2"], p["lam_k2"]]).astype(F32)
    ya = _diff_attention(proj, lamv, p["a_subln_g"].reshape(1, LANES).astype(F32),
                         batch, seq, tq, lam_init)
    yb = _band_attention(proj, _band_bias_mask(p["b_rel_bias"], tq), batch, seq, tq)
    yc = _stick_attention(proj, consts["tri"], batch, seq, tq)

    xn, h2 = _merge(x2d, ya, yb, yc, proj, p["w_branch"].astype(BF16),
                    p["w_out"].astype(BF16), p["norm2_g"].reshape(1, D).astype(F32),
                    consts["tm_merge"])

    wqt = p["peer_wq"].T.astype(BF16)
    sk = p["peer_subkeys"].astype(BF16)
    c1, r2, e1, e2 = _peer_route(h2, wqt, sk)
    u = p["peer_u"].astype(BF16)
    vt = p["peer_v"].T.astype(BF16)
    return _peer_dense(xn, h2, u, vt, c1, r2, e1, e2, consts["tm_peer"], consts["ce"])


def kernel(x, norm1_g, w_in, a_qnorm_g, a_knorm_g, lam_q1, lam_k1, lam_q2, lam_k2, a_subln_g, b_qnorm_g, b_knorm_g, b_rel_bias, w_branch, w_out, norm2_g, peer_wq, peer_subkeys, peer_u, peer_v):
    B, S, D = x.shape
    T = B * S
    depth = w_in.shape[0]
    tq = min(256, S)
    rc, rs1, rs2 = _rope_tables(S)
    seg = np.arange(COL_TILE) // HEAD_DIM
    bd = jnp.asarray(seg[:, None] == seg[None, :], BF16)
    kk = np.arange(tq)
    tri = jnp.asarray(kk[:, None] > kk[None, :], BF16)
    consts = dict(tq=tq, rc=rc, rs1=rs1, rs2=rs2, bd=bd, tri=tri,
                  tm_in=min(1024, S), tm_merge=min(512, T), tm_peer=min(512, T), ce=1024)
    params = dict(norm1_g=norm1_g, w_in=w_in, a_qnorm_g=a_qnorm_g, a_knorm_g=a_knorm_g,
                  lam_q1=lam_q1, lam_k1=lam_k1, lam_q2=lam_q2, lam_k2=lam_k2,
                  a_subln_g=a_subln_g, b_qnorm_g=b_qnorm_g, b_knorm_g=b_knorm_g,
                  b_rel_bias=b_rel_bias, w_branch=w_branch, w_out=w_out, norm2_g=norm2_g,
                  peer_wq=peer_wq, peer_subkeys=peer_subkeys, peer_u=peer_u, peer_v=peer_v)
    x2d = x.reshape(T, D)
    for layer in range(depth):
        x2d = _layer(x2d, {k: v[layer] for k, v in params.items()}, layer, B, S, consts)
    return x2d.reshape(B, S, D)
```

```python
import functools
import math

import numpy as np
import jax
import jax.numpy as jnp
from jax import lax
from jax.experimental import pallas as pl
from jax.experimental.pallas import tpu as pltpu

F32 = jnp.float32
BF16 = jnp.bfloat16

CHUNK = 64
HEAD_DIM = 64
EPS = 1e-6
ROPE_THETA = 500000.0
ROPE_DIMS = HEAD_DIM // 4
A_HEADS = 4
B_HEADS = 8
B_LEFT_CHUNKS = 8
REL_CLIP = 128
C_HEADS = 8
N_BRANCH = 3
PEER_HEADS = 8
N_KEYS = 128
PEER_HALF = 128
PEER_TOPK = 16

LANES = 128
COL_TILE = 512
NEG_BIG = -1e30
LOG2E = math.log2(math.e)
VMEM_LIMIT = 56 * 1024 * 1024

GATE_COLS = 3 * 1024
_BLK = lambda tile: (GATE_COLS + tile * COL_TILE) // LANES
AQ_BLK, AK_BLK, AV_BLK = _BLK(0), _BLK(1), _BLK(2)
BQ_BLK, BK_BLK, BV_BLK = _BLK(3), _BLK(4), _BLK(5)
CQ_BLK, CK_BLK, CV_BLK = _BLK(6), _BLK(7), _BLK(8)

_NT = (((1,), (1,)), ((), ()))


def _cparams(sem, flags=None):
    return pltpu.CompilerParams(dimension_semantics=sem, vmem_limit_bytes=VMEM_LIMIT,
                                flags=flags)


def _inproj_kernel(x_ref, g_ref, w_ref, rc_ref, rs1_ref, rs2_ref, qkg_ref, bd_ref,
                   o_ref, hn_ref):
    j = pl.program_id(1)
    n_gate = GATE_COLS // COL_TILE

    @pl.when(j == 0)
    def _():
        x = x_ref[...]
        ms = jnp.mean(x * x, axis=-1, keepdims=True)
        hn_ref[...] = (x * lax.rsqrt(ms + EPS) * g_ref[...]).astype(BF16)

    acc = jnp.dot(hn_ref[...], w_ref[...], preferred_element_type=F32)

    def seg_norm(y, row):
        sq = y * y
        hi = sq.astype(BF16)
        lo = (sq - hi.astype(F32)).astype(BF16)
        ss = (jnp.dot(hi, bd_ref[...], preferred_element_type=F32)
              + jnp.dot(lo, bd_ref[...], preferred_element_type=F32))
        return y * lax.rsqrt(ss * (1.0 / HEAD_DIM) + EPS) * qkg_ref[pl.ds(row, 1), :]

    def rope(y):
        outs = []
        for c in range(COL_TILE // LANES):
            ys = y[:, c * LANES:(c + 1) * LANES]
            outs.append(ys * rc_ref[...]
                        + pltpu.roll(ys, LANES - ROPE_DIMS // 2, 1) * rs1_ref[...]
                        + pltpu.roll(ys, ROPE_DIMS // 2, 1) * rs2_ref[...])
        return jnp.concatenate(outs, axis=1)

    scale = HEAD_DIM ** -0.5 * LOG2E

    @pl.when(j < n_gate)
    def _():
        o_ref[...] = jax.nn.sigmoid(acc).astype(BF16)

    @pl.when(j == n_gate + 0)
    def _():
        o_ref[...] = (rope(seg_norm(acc, 0)) * scale).astype(BF16)

    @pl.when(j == n_gate + 1)
    def _():
        o_ref[...] = rope(seg_norm(acc, 1)).astype(BF16)

    @pl.when(j == n_gate + 3)
    def _():
        o_ref[...] = (seg_norm(acc, 2) * scale).astype(BF16)

    @pl.when(j == n_gate + 4)
    def _():
        o_ref[...] = seg_norm(acc, 3).astype(BF16)

    @pl.when(j == n_gate + 6)
    def _():
        o_ref[...] = (acc * scale).astype(BF16)

    plain = ((j == n_gate + 2) | (j == n_gate + 5) | (j == n_gate + 7) | (j == n_gate + 8))

    @pl.when(plain)
    def _():
        o_ref[...] = acc.astype(BF16)


def _in_projection(x2d, g1, w, rc, rs1, rs2, qkg, bd, seq, tm):
    T, D = x2d.shape
    ncol = w.shape[1]
    nseq = seq // tm
    return pl.pallas_call(
        _inproj_kernel,
        grid=(T // tm, ncol // COL_TILE),
        in_specs=[
            pl.BlockSpec((tm, D), lambda i, j: (i, 0)),
            pl.BlockSpec((1, D), lambda i, j: (0, 0)),
            pl.BlockSpec((D, COL_TILE), lambda i, j: (0, j)),
            pl.BlockSpec((tm, LANES), lambda i, j: (i % nseq, 0)),
            pl.BlockSpec((tm, LANES), lambda i, j: (i % nseq, 0)),
            pl.BlockSpec((tm, LANES), lambda i, j: (i % nseq, 0)),
            pl.BlockSpec((4, COL_TILE), lambda i, j: (0, 0)),
            pl.BlockSpec((COL_TILE, COL_TILE), lambda i, j: (0, 0)),
        ],
        out_specs=pl.BlockSpec((tm, COL_TILE), lambda i, j: (i, j)),
        out_shape=jax.ShapeDtypeStruct((T, ncol), BF16),
        scratch_shapes=[pltpu.VMEM((tm, D), BF16)],
        compiler_params=_cparams(("parallel", "arbitrary")),
        name="in_projection",
    )(x2d, g1, w, rc, rs1, rs2, qkg, bd)


def _diff_attn_kernel(lam_ref, sg_ref, q_ref, k_ref, v_ref, o_ref, m_ref, l_ref, acc_ref,
                      *, tq, lam_init):
    i = pl.program_id(2)
    q = q_ref[...]
    lane = lax.broadcasted_iota(jnp.int32, (tq, LANES), 1)
    zero = jnp.zeros_like(q)
    qs = jnp.concatenate([jnp.where(lane < HEAD_DIM, q, zero),
                          jnp.where(lane >= HEAD_DIM, q, zero)], axis=0)

    m_ref[...] = jnp.full(m_ref.shape, NEG_BIG, F32)
    l_ref[...] = jnp.zeros(l_ref.shape, F32)
    acc_ref[...] = jnp.zeros(acc_ref.shape, F32)

    nt = tq // LANES

    def step(blocks, masked):
        ss = []
        for j in blocks:
            k = k_ref[pl.ds(pl.multiple_of(j * tq, tq), tq), :]
            s = lax.dot_general(qs, k, _NT, preferred_element_type=F32)
            if masked:
                row = lax.broadcasted_iota(jnp.int32, (2 * tq, tq), 0)
                col = lax.broadcasted_iota(jnp.int32, (2 * tq, tq), 1)
                qrow = jnp.where(row >= tq, row - tq, row)
                shift = CHUNK.bit_length() - 1
                s = jnp.where(jnp.right_shift(col, shift) <= jnp.right_shift(qrow, shift),
                              s, NEG_BIG)
            ss.append(s)
        m_prev = m_ref[...]
        smax = functools.reduce(jnp.maximum, [s[:, c * LANES:(c + 1) * LANES]
                                              for s in ss for c in range(nt)])
        m_new = jnp.maximum(m_prev, jnp.max(smax, axis=-1, keepdims=True))
        alpha = jnp.exp2(m_prev - m_new)
        acc = alpha * acc_ref[...]
        psum = None
        for j, s in zip(blocks, ss):
            ps = [jnp.exp2(s[:, c * LANES:(c + 1) * LANES] - m_new) for c in range(nt)]
            psum = functools.reduce(jnp.add, ps if psum is None else [psum] + ps)
            v = v_ref[pl.ds(pl.multiple_of(j * tq, tq), tq), :]
            acc = acc + jnp.dot(jnp.concatenate(ps, axis=1).astype(BF16), v,
                                preferred_element_type=F32)
        l_ref[...] = alpha * l_ref[...] + jnp.sum(psum, axis=-1, keepdims=True)
        acc_ref[...] = acc
        m_ref[...] = m_new

    def quad_body(t, c):
        step([4 * t + d for d in range(4)], False)
        return c

    lax.fori_loop(0, i // 4, quad_body, 0)
    rem = i % 4

    @pl.when(rem >= 2)
    def _():
        base = i - rem
        step([base, base + 1], False)

    @pl.when(rem % 2 == 1)
    def _():
        step([i - 1], False)

    step([i], True)

    lam = (jnp.exp(jnp.sum(lam_ref[0:1, :] * lam_ref[1:2, :], axis=-1, keepdims=True))
           - jnp.exp(jnp.sum(lam_ref[2:3, :] * lam_ref[3:4, :], axis=-1, keepdims=True))
           + lam_init)
    o = acc_ref[...] / l_ref[...]
    o = o[:tq] - lam * o[tq:]
    ms = jnp.mean(o * o, axis=-1, keepdims=True)
    o = o * lax.rsqrt(ms + EPS) * sg_ref[...]
    o_ref[...] = (o * (1.0 - lam_init)).astype(BF16)


def _diff_attention(proj, lamv, subln_g, batch, seq, tq, lam_init):
    T = proj.shape[0]
    nq = seq // tq
    return pl.pallas_call(
        functools.partial(_diff_attn_kernel, tq=tq, lam_init=lam_init),
        grid=(batch, A_HEADS, nq),
        in_specs=[
            pl.BlockSpec((4, HEAD_DIM), lambda b, h, i: (0, 0)),
            pl.BlockSpec((1, LANES), lambda b, h, i: (0, 0)),
            pl.BlockSpec((tq, LANES), lambda b, h, i: (b * nq + i, AQ_BLK + h)),
            pl.BlockSpec((seq, LANES), lambda b, h, i: (b, AK_BLK + h)),
            pl.BlockSpec((seq, LANES), lambda b, h, i: (b, AV_BLK + h)),
        ],
        out_specs=pl.BlockSpec((tq, LANES), lambda b, h, i: (b * nq + i, h)),
        out_shape=jax.ShapeDtypeStruct((T, A_HEADS * LANES), BF16),
        scratch_shapes=[pltpu.VMEM((2 * tq, LANES), F32), pltpu.VMEM((2 * tq, LANES), F32),
                        pltpu.VMEM((2 * tq, LANES), F32)],
        compiler_params=_cparams(("parallel", "parallel", "arbitrary")),
        name="diff_attention",
    )(lamv, subln_g, proj, proj, proj)


def _band_attn_kernel(bm_ref, q_ref, k_ref, v_ref, o_ref, *, tq):
    i = pl.program_id(2)
    q = q_ref[...]
    lane = lax.broadcasted_iota(jnp.int32, (tq, LANES), 1)
    zero = jnp.zeros_like(q)
    nwin = B_LEFT_CHUNKS * CHUNK // tq + 1
    outs = []
    for half in range(2):
        sel = (lane < HEAD_DIM) if half == 0 else (lane >= HEAD_DIM)
        qh = jnp.where(sel, q, zero)
        ss = []
        for blk in range(nwin):
            kb = i - (nwin - 1) + blk
            kbc = jnp.maximum(kb, 0)
            k = k_ref[pl.ds(pl.multiple_of(kbc * tq, tq), tq), :]
            s = lax.dot_general(qh, k, _NT, preferred_element_type=F32)
            s = s + bm_ref[half, :, blk * tq:(blk + 1) * tq]
            ss.append(jnp.where(kb >= 0, s, NEG_BIG))
        s = jnp.concatenate(ss, axis=1)
        m = jnp.max(s, axis=-1, keepdims=True)
        p = jnp.exp2(s - m)
        l = jnp.sum(p, axis=-1, keepdims=True)
        pb = p.astype(BF16)
        o = jnp.zeros((tq, LANES), F32)
        for blk in range(nwin):
            kbc = jnp.maximum(i - (nwin - 1) + blk, 0)
            v = v_ref[pl.ds(pl.multiple_of(kbc * tq, tq), tq), :]
            o = o + jnp.dot(pb[:, blk * tq:(blk + 1) * tq], v, preferred_element_type=F32)
        outs.append(o / l)
    o_ref[...] = jnp.where(lane < HEAD_DIM, outs[0], outs[1]).astype(BF16)


def _band_attention(proj, biasmask, batch, seq, tq):
    T = proj.shape[0]
    nq = seq // tq
    npair = B_HEADS // 2
    wwin = biasmask.shape[-1]
    return pl.pallas_call(
        functools.partial(_band_attn_kernel, tq=tq),
        grid=(batch, npair, nq),
        in_specs=[
            pl.BlockSpec((2, tq, wwin), lambda b, h, i: (h, 0, 0)),
            pl.BlockSpec((tq, LANES), lambda b, h, i: (b * nq + i, BQ_BLK + h)),
            pl.BlockSpec((seq, LANES), lambda b, h, i: (b, BK_BLK + h)),
            pl.BlockSpec((seq, LANES), lambda b, h, i: (b, BV_BLK + h)),
        ],
        out_specs=pl.BlockSpec((tq, LANES), lambda b, h, i: (b * nq + i, h)),
        out_shape=jax.ShapeDtypeStruct((T, npair * LANES), BF16),
        compiler_params=_cparams(("parallel", "parallel", "arbitrary")),
        name="band_attention",
    )(biasmask, proj, proj, proj)


def _stick_attn_kernel(tri_ref, q_ref, k_ref, v_ref, o_ref, carry_ref, acc_ref, *, tq):
    i = pl.program_id(2)
    q = q_ref[...]
    lane = lax.broadcasted_iota(jnp.int32, (tq, LANES), 1)
    zero = jnp.zeros_like(q)
    qh = (jnp.where(lane < HEAD_DIM, q, zero), jnp.where(lane >= HEAD_DIM, q, zero))

    carry_ref[...] = jnp.zeros(carry_ref.shape, F32)
    acc_ref[...] = jnp.zeros(acc_ref.shape, F32)

    nt = tq // LANES
    lane_tiles = lambda t: [t[:, c * LANES:(c + 1) * LANES] for c in range(nt)]

    def step(blocks, masked):
        if masked:
            row = lax.broadcasted_iota(jnp.int32, (tq, tq), 0)
            col = lax.broadcasted_iota(jnp.int32, (tq, tq), 1)
            strict = col < row
        for half in range(2):
            parts = []
            for j in blocks:
                k = k_ref[pl.ds(pl.multiple_of(j * tq, tq), tq), :]
                z2 = lax.dot_general(qh[half], k, _NT, preferred_element_type=F32)
                nz2 = -z2
                lg2 = jnp.log(1.0 + jnp.exp2(jnp.minimum(z2, nz2))) * LOG2E
                lk = jnp.minimum(nz2, 0.0) - lg2
                ls = lk + z2
                if masked:
                    lk = jnp.where(strict, lk, 0.0)
                later = jnp.dot(lk.astype(BF16), tri_ref[...], preferred_element_type=F32)
                tot = jnp.sum(functools.reduce(jnp.add, lane_tiles(lk)),
                              axis=-1, keepdims=True)
                parts.append((j, ls, later, tot))
            carry = carry_ref[half]
            acc = acc_ref[half]
            for j, ls, later, tot in parts:
                a = jnp.concatenate([jnp.exp2(x + (y + carry))
                                     for x, y in zip(lane_tiles(ls), lane_tiles(later))], axis=1)
                if masked:
                    a = jnp.where(strict, a, 0.0)
                v = v_ref[pl.ds(pl.multiple_of(j * tq, tq), tq), :]
                acc = acc + jnp.dot(a.astype(BF16), v, preferred_element_type=F32)
                carry = carry + tot
            acc_ref[half] = acc
            carry_ref[half] = carry

    step([i], True)

    def quad_body(t, c):
        j = i - 1 - 4 * t
        step([j - d for d in range(4)], False)
        return c

    lax.fori_loop(0, i // 4, quad_body, 0)
    rem = i % 4

    @pl.when(rem >= 2)
    def _():
        step([rem - 1, rem - 2], False)

    @pl.when(rem % 2 == 1)
    def _():
        step([0], False)
    o_ref[...] = jnp.where(lane < HEAD_DIM, acc_ref[0], acc_ref[1]).astype(BF16)


def _stick_attention(proj, tri, batch, seq, tq):
    T = proj.shape[0]
    nq = seq // tq
    npair = C_HEADS // 2
    return pl.pallas_call(
        functools.partial(_stick_attn_kernel, tq=tq),
        grid=(batch, npair, nq),
        in_specs=[
            pl.BlockSpec((tq, tq), lambda b, h, i: (0, 0)),
            pl.BlockSpec((tq, LANES), lambda b, h, i: (b * nq + i, CQ_BLK + h)),
            pl.BlockSpec((seq, LANES), lambda b, h, i: (b, CK_BLK + h)),
            pl.BlockSpec((seq, LANES), lambda b, h, i: (b, CV_BLK + h)),
        ],
        out_specs=pl.BlockSpec((tq, LANES), lambda b, h, i: (b * nq + i, h)),
        out_shape=jax.ShapeDtypeStruct((T, npair * LANES), BF16),
        scratch_shapes=[pltpu.VMEM((2, tq, LANES), F32), pltpu.VMEM((2, tq, LANES), F32)],
        compiler_params=_cparams(("parallel", "parallel", "arbitrary")),
        name="stick_attention",
    )(tri, proj, proj, proj)


def _merge_kernel(x_ref, ya_ref, yb_ref, yc_ref, g0_ref, g1_ref, g2_ref, wb_ref, wo_ref,
                  n2_ref, xo_ref, h2_ref):
    merged = None
    for n, (y_ref, g_ref) in enumerate(((ya_ref, g0_ref), (yb_ref, g1_ref), (yc_ref, g2_ref))):
        up = jnp.dot(y_ref[...], wb_ref[n], preferred_element_type=F32)
        term = g_ref[...].astype(F32) * up
        merged = term if merged is None else merged + term
    xn = x_ref[...] + jnp.dot(merged.astype(BF16), wo_ref[...], preferred_element_type=F32)
    xo_ref[...] = xn
    ms = jnp.mean(xn * xn, axis=-1, keepdims=True)
    h2_ref[...] = (xn * lax.rsqrt(ms + EPS) * n2_ref[...]).astype(BF16)


def _merge(x2d, ya, yb, yc, proj, wb, wo, n2, tm):
    T, D = x2d.shape
    W = ya.shape[1]
    yspec = pl.BlockSpec((tm, W), lambda i: (i, 0))
    return pl.pallas_call(
        _merge_kernel,
        grid=(T // tm,),
        in_specs=[
            pl.BlockSpec((tm, D), lambda i: (i, 0)),
            yspec, yspec, yspec,
            pl.BlockSpec((tm, D), lambda i: (i, 0)),
            pl.BlockSpec((tm, D), lambda i: (i, 1)),
            pl.BlockSpec((tm, D), lambda i: (i, 2)),
            pl.BlockSpec((N_BRANCH, W, D), lambda i: (0, 0, 0)),
            pl.BlockSpec((D, D), lambda i: (0, 0)),
            pl.BlockSpec((1, D), lambda i: (0, 0)),
        ],
        out_specs=[pl.BlockSpec((tm, D), lambda i: (i, 0)),
                   pl.BlockSpec((tm, D), lambda i: (i, 0))],
        out_shape=[jax.ShapeDtypeStruct((T, D), F32), jax.ShapeDtypeStruct((T, D), BF16)],
        compiler_params=_cparams(("parallel",)),
        name="gated_merge",
    )(x2d, ya, yb, yc, proj, proj, proj, wb, wo, n2)


def _candidate_pairs():
    return [(a, b) for a in range(PEER_TOPK) for b in range(PEER_TOPK)
            if (a + 1) * (b + 1) <= PEER_TOPK]


def _peer_route_kernel(h2_ref, wqt_ref, sk_ref, c1_ref, r2_ref, e1_ref, e2_ref,
                       s_ref, xw_ref, rk_ref, top_ref):
    qT = lax.dot_general(wqt_ref[...], h2_ref[...], _NT,
                         preferred_element_type=F32).astype(BF16)
    kio = lax.broadcasted_iota(jnp.int32, (N_KEYS, LANES), 0).astype(F32)
    pairs = _candidate_pairs()

    group = 4
    for h0 in range(0, PEER_HEADS, group):
        probs = [(h, p) for h in range(h0, h0 + group) for p in range(2)]
        for h, p in probs:
            r0 = (h * 2 + p) * PEER_HALF
            x0 = jnp.dot(sk_ref[p, h], qT[r0:r0 + PEER_HALF, :],
                         preferred_element_type=F32)
            s_ref[p, h] = x0
            xw_ref[p, h] = x0
            rk_ref[p, h] = jnp.full((N_KEYS, LANES), float(PEER_TOPK), F32)

        def it(r, c, probs=probs):
            rf = jnp.asarray(r, F32)
            for h, p in probs:
                x = xw_ref[p, h]
                m = jnp.max(x, axis=0, keepdims=True)
                idx = jnp.where(x == m, kio, float(N_KEYS))
                first = jnp.min(idx, axis=0, keepdims=True)
                hit = kio == first
                top_ref[p, r, pl.ds(h, 1), :] = m
                xw_ref[p, h] = jnp.where(hit, -jnp.inf, x)
                rk_ref[p, h] = jnp.where(hit, rf, rk_ref[p, h])
            return c

        lax.fori_loop(0, PEER_TOPK, it, 0)
    ranks = {h: rk_ref[0, h] for h in range(PEER_HEADS)}
    for h in range(PEER_HEADS):
        r2_ref[h] = rk_ref[1, h].astype(BF16)

    a = [top_ref[0, r] for r in range(PEER_TOPK)]
    b = [top_ref[1, r] for r in range(PEER_TOPK)]
    cand = [a[ia] + b[ib] for (ia, ib) in pairs]
    m0 = cand[0]
    cnt = [jnp.zeros_like(m0) for _ in range(PEER_TOPK)]
    z = jnp.zeros_like(m0)
    for _ in range(PEER_TOPK):
        m = functools.reduce(jnp.maximum, cand)
        z = z + jnp.exp(m - m0)
        found = jnp.zeros(m0.shape, jnp.bool_)
        for ci, (ia, ib) in enumerate(pairs):
            hit = (cand[ci] == m) & jnp.logical_not(found)
            found = found | hit
            cand[ci] = jnp.where(hit, -jnp.inf, cand[ci])
            cnt[ia] = cnt[ia] + jnp.where(hit, 1.0, 0.0)
    inv_z = 1.0 / z

    for h in range(PEER_HEADS):
        c1 = jnp.zeros((N_KEYS, LANES), F32)
        for r in range(PEER_TOPK):
            c1 = jnp.where(ranks[h] == float(r), cnt[r][h:h + 1, :], c1)
        c1_ref[h] = c1
        e1_ref[h] = jnp.exp(s_ref[0, h] - a[0][h:h + 1, :])
        e2_ref[h] = (jnp.exp(s_ref[1, h] - b[0][h:h + 1, :]) * inv_z[h:h + 1, :]).astype(BF16)


def _peer_route(h2, wqt, sk):
    T, D = h2.shape
    oshape_w = jax.ShapeDtypeStruct((PEER_HEADS, N_KEYS, T), F32)
    oshape_bf = jax.ShapeDtypeStruct((PEER_HEADS, N_KEYS, T), BF16)
    ospec = pl.BlockSpec((PEER_HEADS, N_KEYS, LANES), lambda i: (0, 0, i))
    return pl.pallas_call(
        _peer_route_kernel,
        grid=(T // LANES,),
        in_specs=[
            pl.BlockSpec((LANES, D), lambda i: (i, 0)),
            pl.BlockSpec(wqt.shape, lambda i: (0, 0)),
            pl.BlockSpec(sk.shape, lambda i: (0, 0, 0, 0)),
        ],
        out_specs=[ospec, ospec, ospec, ospec],
        out_shape=[oshape_w, oshape_bf, oshape_w, oshape_bf],
        scratch_shapes=[pltpu.VMEM((2, PEER_HEADS, N_KEYS, LANES), F32),
                        pltpu.VMEM((2, PEER_HEADS, N_KEYS, LANES), F32),
                        pltpu.VMEM((2, PEER_HEADS, N_KEYS, LANES), F32),
                        pltpu.VMEM((2, PEER_TOPK, PEER_HEADS, LANES), F32)],
        compiler_params=_cparams(("parallel",)),
        name="peer_route",
    )(h2, wqt, sk)


def _peer_dense_kernel(x_ref, h2_ref, u_ref, vt_ref, c1_ref, r2_ref, e1_ref, e2_ref,
                       o_ref, acc_ref, ga_ref, gb_ref, w_ref, cw_ref, ew_ref, *, tm, ce):
    c = pl.program_id(1)
    pack = 16
    span = 4 * pack
    nt = tm // LANES
    n_i1 = ce // N_KEYS

    @pl.when(c == 0)
    def _():
        acc_ref[...] = jnp.zeros(acc_ref.shape, F32)
        gb_ref[...] = jnp.zeros(gb_ref.shape, BF16)

    def bcast_bf16(ref, h, row):
        tiles = []
        for t in range(nt):
            x = ref[pl.ds((h * nt + t) * n_i1 + row, pack // 2, stride=0), :]
            tiles.append(jnp.concatenate([x, x], axis=0).astype(BF16))
        return jnp.concatenate(tiles, axis=1)

    def step(g_cur, g_prev):
        for h in range(PEER_HEADS):
            for t in range(nt):
                slab = pl.ds((h * nt + t) * n_i1, n_i1)
                cw_ref[slab, :] = c1_ref[h, :, t * LANES:(t + 1) * LANES]
                ew_ref[slab, :] = e1_ref[h, :, t * LANES:(t + 1) * LANES]

        zero = jnp.zeros((pack, tm), BF16)
        for ii in range(n_i1):
            for r0 in range(0, N_KEYS, span):
                w = [zero for _ in range(span // pack)]
                for h in range(PEER_HEADS):
                    c1b = bcast_bf16(cw_ref, h, ii)
                    e1b = bcast_bf16(ew_ref, h, ii)
                    for g in range(span // pack):
                        rows = slice(r0 + g * pack, r0 + (g + 1) * pack)
                        w[g] = w[g] + e1b * jnp.where(r2_ref[h, rows, :] < c1b,
                                                      e2_ref[h, rows, :], zero)
                e0 = ii * N_KEYS + r0
                w_ref[e0:e0 + span, :] = jnp.concatenate(w, axis=0)

        hT = lax.dot_general(u_ref[...], h2_ref[...], _NT,
                             preferred_element_type=F32)
        for e0 in range(0, ce, span):
            hh = hT[e0:e0 + span, :]
            act = (0.5 * hh * (1.0 + lax.erf(hh * (1.0 / math.sqrt(2.0))))).astype(BF16)
            g_cur[e0:e0 + span, :] = w_ref[e0:e0 + span, :] * act

        acc_ref[...] += jnp.dot(vt_ref[...], g_prev[...], preferred_element_type=F32)

    @pl.when(lax.rem(c, 2) == 0)
    def _():
        step(ga_ref, gb_ref)

    @pl.when(lax.rem(c, 2) == 1)
    def _():
        step(gb_ref, ga_ref)

    @pl.when(c == pl.num_programs(1) - 1)
    def _():
        o_ref[...] = x_ref[...] + acc_ref[...].T


def _peer_dense(x2d, h2, u, vt, c1, r2, e1, e2, tm, ce):
    T, D = x2d.shape
    ne = u.shape[0]
    nchunk = ne // ce
    build = lambda c: jnp.minimum(c, nchunk - 1)
    drain = lambda c: jnp.maximum(c - 1, 0)
    rspec = pl.BlockSpec((PEER_HEADS, N_KEYS, tm), lambda i, c: (0, 0, i))
    cspec = pl.BlockSpec((PEER_HEADS, ce // N_KEYS, tm), lambda i, c: (0, build(c), i))
    return pl.pallas_call(
        functools.partial(_peer_dense_kernel, tm=tm, ce=ce),
        grid=(T // tm, nchunk + 1),
        in_specs=[
            pl.BlockSpec((tm, D), lambda i, c: (i, 0)),
            pl.BlockSpec((tm, D), lambda i, c: (i, 0)),
            pl.BlockSpec((ce, D), lambda i, c: (build(c), 0)),
            pl.BlockSpec((D, ce), lambda i, c: (0, drain(c))),
            cspec, rspec, cspec, rspec,
        ],
        out_specs=pl.BlockSpec((tm, D), lambda i, c: (i, 0)),
        out_shape=jax.ShapeDtypeStruct((T, D), F32),
        scratch_shapes=[pltpu.VMEM((D, tm), F32),
                        pltpu.VMEM((ce, tm), BF16), pltpu.VMEM((ce, tm), BF16),
                        pltpu.VMEM((ce, tm), BF16),
                        pltpu.VMEM((PEER_HEADS * (tm // LANES) * (ce // N_KEYS), LANES), F32),
                        pltpu.VMEM((PEER_HEADS * (tm // LANES) * (ce // N_KEYS), LANES), F32)],
        compiler_params=_cparams(("parallel", "arbitrary")),
        name="peer_dense",
    )(x2d, h2, u, vt, c1, r2, e1, e2)


def _rope_tables(seq):
    inv = 1.0 / (ROPE_THETA ** (jnp.arange(0, ROPE_DIMS, 2, dtype=F32) / ROPE_DIMS))
    ang = jnp.arange(seq, dtype=F32)[:, None] * inv[None, :]
    cos, sin = jnp.cos(ang), jnp.sin(ang)
    half = ROPE_DIMS // 2
    pad = HEAD_DIM - ROPE_DIMS
    one = jnp.ones((seq, pad), F32)
    zer = jnp.zeros((seq, pad), F32)
    zh = jnp.zeros((seq, half), F32)
    rc = jnp.concatenate([cos, cos, one], axis=1)
    rs1 = jnp.concatenate([-sin, zh, zer], axis=1)
    rs2 = jnp.concatenate([zh, sin, zer], axis=1)
    rep = LANES // HEAD_DIM
    return jnp.tile(rc, (1, rep)), jnp.tile(rs1, (1, rep)), jnp.tile(rs2, (1, rep))


def _band_bias_mask(rel_bias, tq):
    nwin = B_LEFT_CHUNKS * CHUNK // tq + 1
    r = np.arange(tq)[:, None]
    c = np.arange(nwin * tq)[None, :]
    kchunk = c // CHUNK - (nwin - 1) * (tq // CHUNK)
    qchunk = r // CHUNK
    valid = (kchunk <= qchunk) & (kchunk >= qchunk - B_LEFT_CHUNKS)
    wwin, d0 = nwin * tq, (nwin - 1) * tq
    period = wwin + tq
    u = np.arange(-(tq - 1), wwin)
    perm = np.zeros(period, np.int32)
    perm[u % period] = np.clip(d0 - u, -REL_CLIP, REL_CLIP) + REL_CLIP
    line = rel_bias.astype(F32)[:, perm]
    nh = rel_bias.shape[0]
    skew = jnp.broadcast_to(line[:, None, :], (nh, tq, period)).reshape(nh, tq * period)
    bias = skew[:, :tq * (period - 1)].reshape(nh, tq, period - 1)[:, :, :wwin]
    return jnp.where(jnp.asarray(valid)[None], bias * LOG2E, NEG_BIG)


def _layer(x2d, p, layer, batch, seq, consts):
    T, D = x2d.shape
    lam_init = 0.8 - 0.6 * math.exp(-0.3 * layer)
    tq = consts["tq"]

    w_in = p["w_in"]
    n_qkv = w_in.shape[1] - GATE_COLS
    w_perm = jnp.concatenate([w_in[:, n_qkv:], w_in[:, :n_qkv]], axis=1).astype(BF16)
    tile_g = lambda g: jnp.tile(g.astype(F32), COL_TILE // HEAD_DIM)
    qkg = jnp.stack([tile_g(p["a_qnorm_g"]), tile_g(p["a_knorm_g"]),
                     tile_g(p["b_qnorm_g"]), tile_g(p["b_knorm_g"])])
    proj = _in_projection(x2d, p["norm1_g"].reshape(1, D).astype(F32), w_perm,
                          consts["rc"], consts["rs1"], consts["rs2"], qkg, consts["bd"],
                          seq, consts["tm_in"])

    lamv = jnp.stack([p["lam_q1"], p["lam_k1"], p["lam_q2"], p["lam_k2"]]).astype(F32)
    ya = _diff_attention(proj, lamv, p["a_subln_g"].reshape(1, LANES).astype(F32),
                         batch, seq, tq, lam_init)
    yb = _band_attention(proj, _band_bias_mask(p["b_rel_bias"], tq), batch, seq, tq)
    yc = _stick_attention(proj, consts["tri"], batch, seq, tq)

    xn, h2 = _merge(x2d, ya, yb, yc, proj, p["w_branch"].astype(BF16),
                    p["w_out"].astype(BF16), p["norm2_g"].reshape(1, D).astype(F32),
                    consts["tm_merge"])

    wqt = p["peer_wq"].T.astype(BF16)
    sk = p["peer_subkeys"].astype(BF16)
    c1, r2, e1, e2 = _peer_route(h2, wqt, sk)
    u = p["peer_u"].astype(BF16)
    vt = p["peer_v"].T.astype(BF16)
    return _peer_dense(xn, h2, u, vt, c1, r2, e1, e2, consts["tm_peer"], consts["ce"])


def kernel(x, norm1_g, w_in, a_qnorm_g, a_knorm_g, lam_q1, lam_k1, lam_q2, lam_k2, a_subln_g, b_qnorm_g, b_knorm_g, b_rel_bias, w_branch, w_out, norm2_g, peer_wq, peer_subkeys, peer_u, peer_v):
    B, S, D = x.shape
    T = B * S
    depth = w_in.shape[0]
    tq = min(256, S)
    rc, rs1, rs2 = _rope_tables(S)
    seg = np.arange(COL_TILE) // HEAD_DIM
    bd = jnp.asarray(seg[:, None] == seg[None, :], BF16)
    kk = np.arange(tq)
    tri = jnp.asarray(kk[:, None] > kk[None, :], BF16)
    consts = dict(tq=tq, rc=rc, rs1=rs1, rs2=rs2, bd=bd, tri=tri,
                  tm_in=min(1024, S), tm_merge=min(512, T), tm_peer=min(512, T), ce=1024)
    params = dict(norm1_g=norm1_g, w_in=w_in, a_qnorm_g=a_qnorm_g, a_knorm_g=a_knorm_g,
                  lam_q1=lam_q1, lam_k1=lam_k1, lam_q2=lam_q2, lam_k2=lam_k2,
                  a_subln_g=a_subln_g, b_qnorm_g=b_qnorm_g, b_knorm_g=b_knorm_g,
                  b_rel_bias=b_rel_bias, w_branch=w_branch, w_out=w_out, norm2_g=norm2_g,
                  peer_wq=peer_wq, peer_subkeys=peer_subkeys, peer_u=peer_u, peer_v=peer_v)
    x2d = x.reshape(T, D)
    for layer in range(depth):
        x2d = _layer(x2d, {k: v[layer] for k, v in params.items()}, layer, B, S, consts)
    return x2d.reshape(B, S, D)
```

```python
import functools
import math

import numpy as np
import jax
import jax.numpy as jnp
from jax import lax
from jax.experimental import pallas as pl
from jax.experimental.pallas import tpu as pltpu

F32 = jnp.float32
BF16 = jnp.bfloat16

CHUNK = 64
HEAD_DIM = 64
EPS = 1e-6
ROPE_THETA = 500000.0
ROPE_DIMS = HEAD_DIM // 4
A_HEADS = 4
B_HEADS = 8
B_LEFT_CHUNKS = 8
REL_CLIP = 128
C_HEADS = 8
N_BRANCH = 3
PEER_HEADS = 8
N_KEYS = 128
PEER_HALF = 128
PEER_TOPK = 16

LANES = 128
COL_TILE = 512
NEG_BIG = -1e30
LOG2E = math.log2(math.e)
VMEM_LIMIT = 56 * 1024 * 1024

GATE_COLS = 3 * 1024
_BLK = lambda tile: (GATE_COLS + tile * COL_TILE) // LANES
AQ_BLK, AK_BLK, AV_BLK = _BLK(0), _BLK(1), _BLK(2)
BQ_BLK, BK_BLK, BV_BLK = _BLK(3), _BLK(4), _BLK(5)
CQ_BLK, CK_BLK, CV_BLK = _BLK(6), _BLK(7), _BLK(8)

_NT = (((1,), (1,)), ((), ()))


def _cparams(sem, flags=None):
    return pltpu.CompilerParams(dimension_semantics=sem, vmem_limit_bytes=VMEM_LIMIT,
                                flags=flags)


def _inproj_kernel(x_ref, g_ref, w_ref, rc_ref, rs1_ref, rs2_ref, qkg_ref, bd_ref,
                   o_ref, hn_ref):
    j = pl.program_id(1)
    n_gate = GATE_COLS // COL_TILE

    @pl.when(j == 0)
    def _():
        x = x_ref[...]
        ms = jnp.mean(x * x, axis=-1, keepdims=True)
        hn_ref[...] = (x * lax.rsqrt(ms + EPS) * g_ref[...]).astype(BF16)

    acc = jnp.dot(hn_ref[...], w_ref[...], preferred_element_type=F32)

    def seg_norm(y, row):
        sq = y * y
        hi = sq.astype(BF16)
        lo = (sq - hi.astype(F32)).astype(BF16)
        ss = (jnp.dot(hi, bd_ref[...], preferred_element_type=F32)
              + jnp.dot(lo, bd_ref[...], preferred_element_type=F32))
        return y * lax.rsqrt(ss * (1.0 / HEAD_DIM) + EPS) * qkg_ref[pl.ds(row, 1), :]

    def rope(y):
        outs = []
        for c in range(COL_TILE // LANES):
            ys = y[:, c * LANES:(c + 1) * LANES]
            outs.append(ys * rc_ref[...]
                        + pltpu.roll(ys, LANES - ROPE_DIMS // 2, 1) * rs1_ref[...]
                        + pltpu.roll(ys, ROPE_DIMS // 2, 1) * rs2_ref[...])
        return jnp.concatenate(outs, axis=1)

    scale = HEAD_DIM ** -0.5 * LOG2E

    @pl.when(j < n_gate)
    def _():
        o_ref[...] = jax.nn.sigmoid(acc).astype(BF16)

    @pl.when(j == n_gate + 0)
    def _():
        o_ref[...] = (rope(seg_norm(acc, 0)) * scale).astype(BF16)

    @pl.when(j == n_gate + 1)
    def _():
        o_ref[...] = rope(seg_norm(acc, 1)).astype(BF16)

    @pl.when(j == n_gate + 3)
    def _():
        o_ref[...] = (seg_norm(acc, 2) * scale).astype(BF16)

    @pl.when(j == n_gate + 4)
    def _():
        o_ref[...] = seg_norm(acc, 3).astype(BF16)

    @pl.when(j == n_gate + 6)
    def _():
        o_ref[...] = (acc * scale).astype(BF16)

    plain = ((j == n_gate + 2) | (j == n_gate + 5) | (j == n_gate + 7) | (j == n_gate + 8))

    @pl.when(plain)
    def _():
        o_ref[...] = acc.astype(BF16)


def _in_projection(x2d, g1, w, rc, rs1, rs2, qkg, bd, seq, tm):
    T, D = x2d.shape
    ncol = w.shape[1]
    nseq = seq // tm
    ntile = ncol // COL_TILE
    n_gate = GATE_COLS // COL_TILE
    wtile = lambda j: lax.rem(j + ntile - n_gate, ntile)
    return pl.pallas_call(
        _inproj_kernel,
        grid=(T // tm, ntile),
        in_specs=[
            pl.BlockSpec((tm, D), lambda i, j: (i, 0)),
            pl.BlockSpec((1, D), lambda i, j: (0, 0)),
            pl.BlockSpec((D, COL_TILE), lambda i, j: (0, wtile(j))),
            pl.BlockSpec((tm, LANES), lambda i, j: (i % nseq, 0)),
            pl.BlockSpec((tm, LANES), lambda i, j: (i % nseq, 0)),
            pl.BlockSpec((tm, LANES), lambda i, j: (i % nseq, 0)),
            pl.BlockSpec((4, COL_TILE), lambda i, j: (0, 0)),
            pl.BlockSpec((COL_TILE, COL_TILE), lambda i, j: (0, 0)),
        ],
        out_specs=pl.BlockSpec((tm, COL_TILE), lambda i, j: (i, j)),
        out_shape=jax.ShapeDtypeStruct((T, ncol), BF16),
        scratch_shapes=[pltpu.VMEM((tm, D), BF16)],
        compiler_params=_cparams(("parallel", "arbitrary")),
        name="in_projection",
    )(x2d, g1, w, rc, rs1, rs2, qkg, bd)


def _diff_attn_kernel(lam_ref, sg_ref, q_ref, k_ref, v_ref, o_ref, m_ref, l_ref, acc_ref,
                      *, tq, lam_init):
    i = pl.program_id(2)
    q = q_ref[...]
    lane = lax.broadcasted_iota(jnp.int32, (tq, LANES), 1)
    zero = jnp.zeros_like(q)
    qs = jnp.concatenate([jnp.where(lane < HEAD_DIM, q, zero),
                          jnp.where(lane >= HEAD_DIM, q, zero)], axis=0)

    m_ref[...] = jnp.full(m_ref.shape, NEG_BIG, F32)
    l_ref[...] = jnp.zeros(l_ref.shape, F32)
    acc_ref[...] = jnp.zeros(acc_ref.shape, F32)

    nt = tq // LANES

    def step(blocks, masked):
        ss = []
        for j in blocks:
            k = k_ref[pl.ds(pl.multiple_of(j * tq, tq), tq), :]
            s = lax.dot_general(qs, k, _NT, preferred_element_type=F32)
            if masked:
                row = lax.broadcasted_iota(jnp.int32, (2 * tq, tq), 0)
                col = lax.broadcasted_iota(jnp.int32, (2 * tq, tq), 1)
                qrow = jnp.where(row >= tq, row - tq, row)
                shift = CHUNK.bit_length() - 1
                s = jnp.where(jnp.right_shift(col, shift) <= jnp.right_shift(qrow, shift),
                              s, NEG_BIG)
            ss.append(s)
        m_prev = m_ref[...]
        smax = functools.reduce(jnp.maximum, [s[:, c * LANES:(c + 1) * LANES]
                                              for s in ss for c in range(nt)])
        m_new = jnp.maximum(m_prev, jnp.max(smax, axis=-1, keepdims=True))
        alpha = jnp.exp2(m_prev - m_new)
        acc = alpha * acc_ref[...]
        psum = None
        for j, s in zip(blocks, ss):
            ps = [jnp.exp2(s[:, c * LANES:(c + 1) * LANES] - m_new) for c in range(nt)]
            psum = functools.reduce(jnp.add, ps if psum is None else [psum] + ps)
            v = v_ref[pl.ds(pl.multiple_of(j * tq, tq), tq), :]
            acc = acc + jnp.dot(jnp.concatenate(ps, axis=1).astype(BF16), v,
                                preferred_element_type=F32)
        l_ref[...] = alpha * l_ref[...] + jnp.sum(psum, axis=-1, keepdims=True)
        acc_ref[...] = acc
        m_ref[...] = m_new

    def quad_body(t, c):
        step([4 * t + d for d in range(4)], False)
        return c

    lax.fori_loop(0, i // 4, quad_body, 0)
    rem = i % 4

    @pl.when(rem >= 2)
    def _():
        base = i - rem
        step([base, base + 1], False)

    @pl.when(rem % 2 == 1)
    def _():
        step([i - 1], False)

    step([i], True)

    lam = (jnp.exp(jnp.sum(lam_ref[0:1, :] * lam_ref[1:2, :], axis=-1, keepdims=True))
           - jnp.exp(jnp.sum(lam_ref[2:3, :] * lam_ref[3:4, :], axis=-1, keepdims=True))
           + lam_init)
    o = acc_ref[...] / l_ref[...]
    o = o[:tq] - lam * o[tq:]
    ms = jnp.mean(o * o, axis=-1, keepdims=True)
    o = o * lax.rsqrt(ms + EPS) * sg_ref[...]
    o_ref[...] = (o * (1.0 - lam_init)).astype(BF16)


def _diff_attention(proj, lamv, subln_g, batch, seq, tq, lam_init):
    T = proj.shape[0]
    nq = seq // tq
    return pl.pallas_call(
        functools.partial(_diff_attn_kernel, tq=tq, lam_init=lam_init),
        grid=(batch, A_HEADS, nq),
        in_specs=[
            pl.BlockSpec((4, HEAD_DIM), lambda b, h, i: (0, 0)),
            pl.BlockSpec((1, LANES), lambda b, h, i: (0, 0)),
            pl.BlockSpec((tq, LANES), lambda b, h, i: (b * nq + i, AQ_BLK + h)),
            pl.BlockSpec((seq, LANES), lambda b, h, i: (b, AK_BLK + h)),
            pl.BlockSpec((seq, LANES), lambda b, h, i: (b, AV_BLK + h)),
        ],
        out_specs=pl.BlockSpec((tq, LANES), lambda b, h, i: (b * nq + i, h)),
        out_shape=jax.ShapeDtypeStruct((T, A_HEADS * LANES), BF16),
        scratch_shapes=[pltpu.VMEM((2 * tq, LANES), F32), pltpu.VMEM((2 * tq, LANES), F32),
                        pltpu.VMEM((2 * tq, LANES), F32)],
        compiler_params=_cparams(("parallel", "parallel", "arbitrary")),
        name="diff_attention",
    )(lamv, subln_g, proj, proj, proj)


def _band_attn_kernel(bm_ref, q_ref, k_ref, v_ref, o_ref, *, tq):
    i = pl.program_id(2)
    q = q_ref[...]
    lane = lax.broadcasted_iota(jnp.int32, (tq, LANES), 1)
    zero = jnp.zeros_like(q)
    nwin = B_LEFT_CHUNKS * CHUNK // tq + 1
    outs = []
    for half in range(2):
        sel = (lane < HEAD_DIM) if half == 0 else (lane >= HEAD_DIM)
        qh = jnp.where(sel, q, zero)
        ss = []
        for blk in range(nwin):
            kb = i - (nwin - 1) + blk
            kbc = jnp.maximum(kb, 0)
            k = k_ref[pl.ds(pl.multiple_of(kbc * tq, tq), tq), :]
            s = lax.dot_general(qh, k, _NT, preferred_element_type=F32)
            s = s + bm_ref[half, :, blk * tq:(blk + 1) * tq]
            ss.append(jnp.where(kb >= 0, s, NEG_BIG))
        s = jnp.concatenate(ss, axis=1)
        m = jnp.max(s, axis=-1, keepdims=True)
        p = jnp.exp2(s - m)
        l = jnp.sum(p, axis=-1, keepdims=True)
        pb = p.astype(BF16)
        o = jnp.zeros((tq, LANES), F32)
        for blk in range(nwin):
            kbc = jnp.maximum(i - (nwin - 1) + blk, 0)
            v = v_ref[pl.ds(pl.multiple_of(kbc * tq, tq), tq), :]
            o = o + jnp.dot(pb[:, blk * tq:(blk + 1) * tq], v, preferred_element_type=F32)
        outs.append(o / l)
    o_ref[...] = jnp.where(lane < HEAD_DIM, outs[0], outs[1]).astype(BF16)


def _band_attention(proj, biasmask, batch, seq, tq):
    T = proj.shape[0]
    nq = seq // tq
    npair = B_HEADS // 2
    wwin = biasmask.shape[-1]
    return pl.pallas_call(
        functools.partial(_band_attn_kernel, tq=tq),
        grid=(batch, npair, nq),
        in_specs=[
            pl.BlockSpec((2, tq, wwin), lambda b, h, i: (h, 0, 0)),
            pl.BlockSpec((tq, LANES), lambda b, h, i: (b * nq + i, BQ_BLK + h)),
            pl.BlockSpec((seq, LANES), lambda b, h, i: (b, BK_BLK + h)),
            pl.BlockSpec((seq, LANES), lambda b, h, i: (b, BV_BLK + h)),
        ],
        out_specs=pl.BlockSpec((tq, LANES), lambda b, h, i: (b * nq + i, h)),
        out_shape=jax.ShapeDtypeStruct((T, npair * LANES), BF16),
        compiler_params=_cparams(("parallel", "parallel", "arbitrary")),
        name="band_attention",
    )(biasmask, proj, proj, proj)


def _stick_attn_kernel(tri_ref, q_ref, k_ref, v_ref, o_ref, carry_ref, acc_ref, *, tq):
    i = pl.program_id(2)
    q = q_ref[...]
    lane = lax.broadcasted_iota(jnp.int32, (tq, LANES), 1)
    zero = jnp.zeros_like(q)
    qh = (jnp.where(lane < HEAD_DIM, q, zero), jnp.where(lane >= HEAD_DIM, q, zero))

    carry_ref[...] = jnp.zeros(carry_ref.shape, F32)
    acc_ref[...] = jnp.zeros(acc_ref.shape, F32)

    nt = tq // LANES
    lane_tiles = lambda t: [t[:, c * LANES:(c + 1) * LANES] for c in range(nt)]

    def step(blocks, masked):
        if masked:
            row = lax.broadcasted_iota(jnp.int32, (tq, tq), 0)
            col = lax.broadcasted_iota(jnp.int32, (tq, tq), 1)
            strict = col < row
        for half in range(2):
            parts = []
            for j in blocks:
                k = k_ref[pl.ds(pl.multiple_of(j * tq, tq), tq), :]
                z2 = lax.dot_general(qh[half], k, _NT, preferred_element_type=F32)
                nz2 = -z2
                lg2 = jnp.log(1.0 + jnp.exp2(jnp.minimum(z2, nz2))) * LOG2E
                lk = jnp.minimum(nz2, 0.0) - lg2
                ls = lk + z2
                if masked:
                    lk = jnp.where(strict, lk, 0.0)
                later = jnp.dot(lk.astype(BF16), tri_ref[...], preferred_element_type=F32)
                tot = jnp.sum(functools.reduce(jnp.add, lane_tiles(lk)),
                              axis=-1, keepdims=True)
                parts.append((j, ls, later, tot))
            carry = carry_ref[half]
            acc = acc_ref[half]
            for j, ls, later, tot in parts:
                a = jnp.concatenate([jnp.exp2(x + (y + carry))
                                     for x, y in zip(lane_tiles(ls), lane_tiles(later))], axis=1)
                if masked:
                    a = jnp.where(strict, a, 0.0)
                v = v_ref[pl.ds(pl.multiple_of(j * tq, tq), tq), :]
                acc = acc + jnp.dot(a.astype(BF16), v, preferred_element_type=F32)
                carry = carry + tot
            acc_ref[half] = acc
            carry_ref[half] = carry

    step([i], True)

    def quad_body(t, c):
        j = i - 1 - 4 * t
        step([j - d for d in range(4)], False)
        return c

    lax.fori_loop(0, i // 4, quad_body, 0)
    rem = i % 4

    @pl.when(rem >= 2)
    def _():
        step([rem - 1, rem - 2], False)

    @pl.when(rem % 2 == 1)
    def _():
        step([0], False)
    o_ref[...] = jnp.where(lane < HEAD_DIM, acc_ref[0], acc_ref[1]).astype(BF16)


def _stick_attention(proj, tri, batch, seq, tq):
    T = proj.shape[0]
    nq = seq // tq
    npair = C_HEADS // 2
    return pl.pallas_call(
        functools.partial(_stick_attn_kernel, tq=tq),
        grid=(batch, npair, nq),
        in_specs=[
            pl.BlockSpec((tq, tq), lambda b, h, i: (0, 0)),
            pl.BlockSpec((tq, LANES), lambda b, h, i: (b * nq + i, CQ_BLK + h)),
            pl.BlockSpec((seq, LANES), lambda b, h, i: (b, CK_BLK + h)),
            pl.BlockSpec((seq, LANES), lambda b, h, i: (b, CV_BLK + h)),
        ],
        out_specs=pl.BlockSpec((tq, LANES), lambda b, h, i: (b * nq + i, h)),
        out_shape=jax.ShapeDtypeStruct((T, npair * LANES), BF16),
        scratch_shapes=[pltpu.VMEM((2, tq, LANES), F32), pltpu.VMEM((2, tq, LANES), F32)],
        compiler_params=_cparams(("parallel", "parallel", "arbitrary")),
        name="stick_attention",
    )(tri, proj, proj, proj)


def _merge_kernel(x_ref, ya_ref, yb_ref, yc_ref, g0_ref, g1_ref, g2_ref, wb_ref, wo_ref,
                  n2_ref, xo_ref, h2_ref):
    merged = None
    for n, (y_ref, g_ref) in enumerate(((ya_ref, g0_ref), (yb_ref, g1_ref), (yc_ref, g2_ref))):
        up = jnp.dot(y_ref[...], wb_ref[n], preferred_element_type=F32)
        term = g_ref[...].astype(F32) * up
        merged = term if merged is None else merged + term
    xn = x_ref[...] + jnp.dot(merged.astype(BF16), wo_ref[...], preferred_element_type=F32)
    xo_ref[...] = xn
    ms = jnp.mean(xn * xn, axis=-1, keepdims=True)
    h2_ref[...] = (xn * lax.rsqrt(ms + EPS) * n2_ref[...]).astype(BF16)


def _merge(x2d, ya, yb, yc, proj, wb, wo, n2, tm):
    T, D = x2d.shape
    W = ya.shape[1]
    yspec = pl.BlockSpec((tm, W), lambda i: (i, 0))
    return pl.pallas_call(
        _merge_kernel,
        grid=(T // tm,),
        in_specs=[
            pl.BlockSpec((tm, D), lambda i: (i, 0)),
            yspec, yspec, yspec,
            pl.BlockSpec((tm, D), lambda i: (i, 0)),
            pl.BlockSpec((tm, D), lambda i: (i, 1)),
            pl.BlockSpec((tm, D), lambda i: (i, 2)),
            pl.BlockSpec((N_BRANCH, W, D), lambda i: (0, 0, 0)),
            pl.BlockSpec((D, D), lambda i: (0, 0)),
            pl.BlockSpec((1, D), lambda i: (0, 0)),
        ],
        out_specs=[pl.BlockSpec((tm, D), lambda i: (i, 0)),
                   pl.BlockSpec((tm, D), lambda i: (i, 0))],
        out_shape=[jax.ShapeDtypeStruct((T, D), F32), jax.ShapeDtypeStruct((T, D), BF16)],
        compiler_params=_cparams(("parallel",)),
        name="gated_merge",
    )(x2d, ya, yb, yc, proj, proj, proj, wb, wo, n2)


def _candidate_pairs():
    return [(a, b) for a in range(PEER_TOPK) for b in range(PEER_TOPK)
            if (a + 1) * (b + 1) <= PEER_TOPK]


def _peer_route_kernel(h2_ref, wqt_ref, sk_ref, c1_ref, r2_ref, e1_ref, e2_ref,
                       s_ref, xw_ref, rk_ref, top_ref):
    qT = lax.dot_general(wqt_ref[...], h2_ref[...], _NT,
                         preferred_element_type=F32).astype(BF16)
    kio = lax.broadcasted_iota(jnp.int32, (N_KEYS, LANES), 0).astype(F32)
    pairs = _candidate_pairs()

    group = 4
    for h0 in range(0, PEER_HEADS, group):
        probs = [(h, p) for h in range(h0, h0 + group) for p in range(2)]
        for h, p in probs:
            r0 = (h * 2 + p) * PEER_HALF
            x0 = jnp.dot(sk_ref[p, h], qT[r0:r0 + PEER_HALF, :],
                         preferred_element_type=F32)
            s_ref[p, h] = x0
            xw_ref[p, h] = x0
            rk_ref[p, h] = jnp.full((N_KEYS, LANES), float(PEER_TOPK), F32)

        def it(r, c, probs=probs):
            rf = jnp.asarray(r, F32)
            for h, p in probs:
                x = xw_ref[p, h]
                m = jnp.max(x, axis=0, keepdims=True)
                idx = jnp.where(x == m, kio, float(N_KEYS))
                first = jnp.min(idx, axis=0, keepdims=True)
                hit = kio == first
                top_ref[p, r, pl.ds(h, 1), :] = m
                xw_ref[p, h] = jnp.where(hit, -jnp.inf, x)
                rk_ref[p, h] = jnp.where(hit, rf, rk_ref[p, h])
            return c

        lax.fori_loop(0, PEER_TOPK, it, 0)
    ranks = {h: rk_ref[0, h] for h in range(PEER_HEADS)}
    for h in range(PEER_HEADS):
        r2_ref[h] = rk_ref[1, h].astype(BF16)

    a = [top_ref[0, r] for r in range(PEER_TOPK)]
    b = [top_ref[1, r] for r in range(PEER_TOPK)]
    cand = [a[ia] + b[ib] for (ia, ib) in pairs]
    m0 = cand[0]
    cnt = [jnp.zeros_like(m0) for _ in range(PEER_TOPK)]
    z = jnp.zeros_like(m0)
    for _ in range(PEER_TOPK):
        m = functools.reduce(jnp.maximum, cand)
        z = z + jnp.exp(m - m0)
        found = jnp.zeros(m0.shape, jnp.bool_)
        for ci, (ia, ib) in enumerate(pairs):
            hit = (cand[ci] == m) & jnp.logical_not(found)
            found = found | hit
            cand[ci] = jnp.where(hit, -jnp.inf, cand[ci])
            cnt[ia] = cnt[ia] + jnp.where(hit, 1.0, 0.0)
    inv_z = 0.5 / z

    for h in range(PEER_HEADS):
        c1 = jnp.zeros((N_KEYS, LANES), F32)
        for r in range(PEER_TOPK):
            c1 = jnp.where(ranks[h] == float(r), cnt[r][h:h + 1, :], c1)
        c1_ref[h] = c1
        e1_ref[h] = jnp.exp(s_ref[0, h] - a[0][h:h + 1, :])
        e2_ref[h] = (jnp.exp(s_ref[1, h] - b[0][h:h + 1, :]) * inv_z[h:h + 1, :]).astype(BF16)


def _peer_route(h2, wqt, sk):
    T, D = h2.shape
    oshape_w = jax.ShapeDtypeStruct((PEER_HEADS, N_KEYS, T), F32)
    oshape_bf = jax.ShapeDtypeStruct((PEER_HEADS, N_KEYS, T), BF16)
    ospec = pl.BlockSpec((PEER_HEADS, N_KEYS, LANES), lambda i: (0, 0, i))
    return pl.pallas_call(
        _peer_route_kernel,
        grid=(T // LANES,),
        in_specs=[
            pl.BlockSpec((LANES, D), lambda i: (i, 0)),
            pl.BlockSpec(wqt.shape, lambda i: (0, 0)),
            pl.BlockSpec(sk.shape, lambda i: (0, 0, 0, 0)),
        ],
        out_specs=[ospec, ospec, ospec, ospec],
        out_shape=[oshape_w, oshape_bf, oshape_w, oshape_bf],
        scratch_shapes=[pltpu.VMEM((2, PEER_HEADS, N_KEYS, LANES), F32),
                        pltpu.VMEM((2, PEER_HEADS, N_KEYS, LANES), F32),
                        pltpu.VMEM((2, PEER_HEADS, N_KEYS, LANES), F32),
                        pltpu.VMEM((2, PEER_TOPK, PEER_HEADS, LANES), F32)],
        compiler_params=_cparams(("parallel",)),
        name="peer_route",
    )(h2, wqt, sk)


def _peer_dense_kernel(x_ref, h2_ref, u_ref, vt_ref, c1_ref, r2_ref, e1_ref, e2_ref,
                       o_ref, acc_ref, g_ref, w_ref, h_ref, cw_ref, ew_ref, *, tm, ce, nchunk):
    c = pl.program_id(0)
    pack = 16
    span = 4 * pack
    nt = tm // LANES
    n_i1 = ce // N_KEYS

    @pl.when(c == 0)
    def _():
        acc_ref[...] = jnp.zeros(acc_ref.shape, F32)
        g_ref[1] = jnp.zeros(g_ref.shape[1:], BF16)

    def bcast_bf16(ref, h, row):
        tiles = []
        for t in range(nt):
            x = ref[pl.ds((h * nt + t) * n_i1 + row, pack // 2, stride=0), :]
            tiles.append(jnp.concatenate([x, x], axis=0).astype(BF16))
        return jnp.concatenate(tiles, axis=1)

    cur = lax.rem(c, 2)

    for h in range(PEER_HEADS):
        for t in range(nt):
            slab = pl.ds((h * nt + t) * n_i1, n_i1)
            cw_ref[slab, :] = c1_ref[h, :, t * LANES:(t + 1) * LANES]
            ew_ref[slab, :] = e1_ref[h, :, t * LANES:(t + 1) * LANES]

    zero = jnp.zeros((pack, tm), BF16)

    def w_build(ii):
        for r0 in range(0, N_KEYS, span):
            w = [zero for _ in range(span // pack)]
            for h in range(PEER_HEADS):
                c1b = bcast_bf16(cw_ref, h, ii)
                e1b = bcast_bf16(ew_ref, h, ii)
                for g in range(span // pack):
                    rows = slice(r0 + g * pack, r0 + (g + 1) * pack)
                    w[g] = w[g] + e1b * jnp.where(r2_ref[h, rows, :] < c1b,
                                                  e2_ref[h, rows, :], zero)
            e0 = ii * N_KEYS + r0
            w_ref[e0:e0 + span, :] = jnp.concatenate(w, axis=0)

    half = tm // 2
    pieces = []
    for s in range(2):
        tok = slice(s * half, (s + 1) * half)
        pieces.append(("drain", tok))
    for s in range(2):
        tok = slice(s * half, (s + 1) * half)
        pieces.append(("score", tok))
    per = n_i1 // len(pieces)
    for n, (kind, tok) in enumerate(pieces):
        @pl.when(c >= 0)
        def _(n=n, kind=kind, tok=tok):
            if kind == "drain":
                acc_ref[:, tok] += jnp.dot(vt_ref[...], g_ref[1 - cur, :, tok],
                                           preferred_element_type=F32)
            else:
                h_ref[:, tok] = lax.dot_general(u_ref[...], h2_ref[tok, :], _NT,
                                                preferred_element_type=F32)
            for ii in range(n * per, (n + 1) * per):
                w_build(ii)

    def g_body(k, carry):
        rows = pl.ds(pl.multiple_of(k * span, span), span)
        hh = h_ref[rows, :]
        act = (hh + hh * lax.erf(hh * (1.0 / math.sqrt(2.0)))).astype(BF16)
        g_ref[cur, rows, :] = w_ref[rows, :] * act
        return carry

    lax.fori_loop(0, ce // span, g_body, 0)

    @pl.when((c > 0) & (lax.rem(c, nchunk) == 0))
    def _():
        o_ref[...] = x_ref[...] + acc_ref[...].T
        acc_ref[...] = jnp.zeros(acc_ref.shape, F32)


def _peer_dense(x2d, h2, u, vt, c1, r2, e1, e2, tm, ce):
    T, D = x2d.shape
    ne = u.shape[0]
    nchunk = ne // ce
    nitem = (T // tm) * nchunk
    build = lambda c: jnp.minimum(c, nitem - 1)
    drain = lambda c: jnp.maximum(c - 1, 0)
    tile = lambda item: item // nchunk
    chunk = lambda item: lax.rem(item, nchunk)
    rspec = pl.BlockSpec((PEER_HEADS, N_KEYS, tm), lambda c: (0, 0, tile(build(c))))
    cspec = pl.BlockSpec((PEER_HEADS, ce // N_KEYS, tm),
                         lambda c: (0, chunk(build(c)), tile(build(c))))
    return pl.pallas_call(
        functools.partial(_peer_dense_kernel, tm=tm, ce=ce, nchunk=nchunk),
        grid=(nitem + 1,),
        in_specs=[
            pl.BlockSpec((tm, D), lambda c: (tile(drain(c)), 0)),
            pl.BlockSpec((tm, D), lambda c: (tile(build(c)), 0)),
            pl.BlockSpec((ce, D), lambda c: (chunk(build(c)), 0)),
            pl.BlockSpec((None, D, ce), lambda c: (chunk(drain(c)), 0, 0)),
            cspec, rspec, cspec, rspec,
        ],
        out_specs=pl.BlockSpec((tm, D), lambda c: (tile(drain(c)), 0)),
        out_shape=jax.ShapeDtypeStruct((T, D), F32),
        scratch_shapes=[pltpu.VMEM((D, tm), F32),
                        pltpu.VMEM((2, ce, tm), BF16),
                        pltpu.VMEM((ce, tm), BF16),
                        pltpu.VMEM((ce, tm), F32),
                        pltpu.VMEM((PEER_HEADS * (tm // LANES) * (ce // N_KEYS), LANES), F32),
                        pltpu.VMEM((PEER_HEADS * (tm // LANES) * (ce // N_KEYS), LANES), F32)],
        compiler_params=_cparams(("arbitrary",)),
        name="peer_dense",
    )(x2d, h2, u, vt, c1, r2, e1, e2)


def _rope_tables(seq):
    inv = 1.0 / (ROPE_THETA ** (jnp.arange(0, ROPE_DIMS, 2, dtype=F32) / ROPE_DIMS))
    ang = jnp.arange(seq, dtype=F32)[:, None] * inv[None, :]
    cos, sin = jnp.cos(ang), jnp.sin(ang)
    half = ROPE_DIMS // 2
    pad = HEAD_DIM - ROPE_DIMS
    one = jnp.ones((seq, pad), F32)
    zer = jnp.zeros((seq, pad), F32)
    zh = jnp.zeros((seq, half), F32)
    rc = jnp.concatenate([cos, cos, one], axis=1)
    rs1 = jnp.concatenate([-sin, zh, zer], axis=1)
    rs2 = jnp.concatenate([zh, sin, zer], axis=1)
    rep = LANES // HEAD_DIM
    return jnp.tile(rc, (1, rep)), jnp.tile(rs1, (1, rep)), jnp.tile(rs2, (1, rep))


def _band_bias_mask(rel_bias, tq):
    nwin = B_LEFT_CHUNKS * CHUNK // tq + 1
    r = np.arange(tq)[:, None]
    c = np.arange(nwin * tq)[None, :]
    kchunk = c // CHUNK - (nwin - 1) * (tq // CHUNK)
    qchunk = r // CHUNK
    valid = (kchunk <= qchunk) & (kchunk >= qchunk - B_LEFT_CHUNKS)
    wwin, d0 = nwin * tq, (nwin - 1) * tq
    period = wwin + tq
    u = np.arange(-(tq - 1), wwin)
    perm = np.zeros(period, np.int32)
    perm[u % period] = np.clip(d0 - u, -REL_CLIP, REL_CLIP) + REL_CLIP
    line = rel_bias.astype(F32)[:, perm]
    nh = rel_bias.shape[0]
    skew = jnp.broadcast_to(line[:, None, :], (nh, tq, period)).reshape(nh, tq * period)
    bias = skew[:, :tq * (period - 1)].reshape(nh, tq, period - 1)[:, :, :wwin]
    return jnp.where(jnp.asarray(valid)[None], bias * LOG2E, NEG_BIG)


def _layer(x2d, p, layer, batch, seq, consts):
    T, D = x2d.shape
    lam_init = 0.8 - 0.6 * math.exp(-0.3 * layer)
    tq = consts["tq"]

    w_perm = p["w_in"].astype(BF16)
    tile_g = lambda g: jnp.tile(g.astype(F32), COL_TILE // HEAD_DIM)
    qkg = jnp.stack([tile_g(p["a_qnorm_g"]), tile_g(p["a_knorm_g"]),
                     tile_g(p["b_qnorm_g"]), tile_g(p["b_knorm_g"])])
    proj = _in_projection(x2d, p["norm1_g"].reshape(1, D).astype(F32), w_perm,
                          consts["rc"], consts["rs1"], consts["rs2"], qkg, consts["bd"],
                          seq, consts["tm_in"])

    lamv = jnp.stack([p["lam_q1"], p["lam_k1"], p["lam_q2"], p["lam_k2"]]).astype(F32)
    ya = _diff_attention(proj, lamv, p["a_subln_g"].reshape(1, LANES).astype(F32),
                         batch, seq, tq, lam_init)
    yb = _band_attention(proj, _band_bias_mask(p["b_rel_bias"], tq), batch, seq, tq)
    yc = _stick_attention(proj, consts["tri"], batch, seq, tq)

    xn, h2 = _merge(x2d, ya, yb, yc, proj, p["w_branch"].astype(BF16),
                    p["w_out"].astype(BF16), p["norm2_g"].reshape(1, D).astype(F32),
                    consts["tm_merge"])

    wqt = p["peer_wq"].T.astype(BF16)
    sk = p["peer_subkeys"].astype(BF16)
    c1, r2, e1, e2 = _peer_route(h2, wqt, sk)
    u = p["peer_u"].astype(BF16)
    ce = consts["ce"]
    vt = jnp.transpose(p["peer_v"].astype(BF16).reshape(-1, ce, D), (0, 2, 1))
    return _peer_dense(xn, h2, u, vt, c1, r2, e1, e2, consts["tm_peer"], consts["ce"])


def kernel(x, norm1_g, w_in, a_qnorm_g, a_knorm_g, lam_q1, lam_k1, lam_q2, lam_k2, a_subln_g, b_qnorm_g, b_knorm_g, b_rel_bias, w_branch, w_out, norm2_g, peer_wq, peer_subkeys, peer_u, peer_v):
    B, S, D = x.shape
    T = B * S
    depth = w_in.shape[0]
    tq = min(256, S)
    rc, rs1, rs2 = _rope_tables(S)
    seg = np.arange(COL_TILE) // HEAD_DIM
    bd = jnp.asarray(seg[:, None] == seg[None, :], BF16)
    kk = np.arange(tq)
    tri = jnp.asarray(kk[:, None] > kk[None, :], BF16)
    consts = dict(tq=tq, rc=rc, rs1=rs1, rs2=rs2, bd=bd, tri=tri,
                  tm_in=min(1024, S), tm_merge=min(512, T), tm_peer=min(512, T), ce=1024)
    params = dict(norm1_g=norm1_g, w_in=w_in, a_qnorm_g=a_qnorm_g, a_knorm_g=a_knorm_g,
                  lam_q1=lam_q1, lam_k1=lam_k1, lam_q2=lam_q2, lam_k2=lam_k2,
                  a_subln_g=a_subln_g, b_qnorm_g=b_qnorm_g, b_knorm_g=b_knorm_g,
                  b_rel_bias=b_rel_bias, w_branch=w_branch, w_out=w_out, norm2_g=norm2_g,
                  peer_wq=peer_wq, peer_subkeys=peer_subkeys, peer_u=peer_u, peer_v=peer_v)
    x2d = x.reshape(T, D)
    for layer in range(depth):
        x2d = _layer(x2d, {k: v[layer] for k, v in params.items()}, layer, B, S, consts)
    return x2d.reshape(B, S, D)
```

```python
import functools
import math

import numpy as np
import jax
import jax.numpy as jnp
from jax import lax
from jax.experimental import pallas as pl
from jax.experimental.pallas import tpu as pltpu

F32 = jnp.float32
BF16 = jnp.bfloat16

CHUNK = 64
HEAD_DIM = 64
EPS = 1e-6
ROPE_THETA = 500000.0
ROPE_DIMS = HEAD_DIM // 4
A_HEADS = 4
B_HEADS = 8
B_LEFT_CHUNKS = 8
REL_CLIP = 128
C_HEADS = 8
N_BRANCH = 3
PEER_HEADS = 8
N_KEYS = 128
PEER_HALF = 128
PEER_TOPK = 16

LANES = 128
COL_TILE = 512
NEG_BIG = -1e30
LOG2E = math.log2(math.e)
KV_UNROLL = 4
LAST_GROUP = 4
VMEM_LIMIT = 56 * 1024 * 1024

GATE_COLS = 3 * 1024
_BLK = lambda tile: (GATE_COLS + tile * COL_TILE) // LANES
AQ_BLK, AK_BLK, AV_BLK = _BLK(0), _BLK(1), _BLK(2)
BQ_BLK, BK_BLK, BV_BLK = _BLK(3), _BLK(4), _BLK(5)
CQ_BLK, CK_BLK, CV_BLK = _BLK(6), _BLK(7), _BLK(8)

_NT = (((1,), (1,)), ((), ()))


def _cparams(sem, flags=None):
    return pltpu.CompilerParams(dimension_semantics=sem, vmem_limit_bytes=VMEM_LIMIT,
                                flags=flags)


def _inproj_kernel(x_ref, g_ref, w_ref, rc_ref, rs1_ref, rs2_ref, qkg_ref, bd_ref,
                   o_ref, hn_ref):
    j = pl.program_id(1)
    n_gate = GATE_COLS // COL_TILE

    @pl.when(j == 0)
    def _():
        x = x_ref[...]
        ms = jnp.mean(x * x, axis=-1, keepdims=True)
        hn_ref[...] = (x * lax.rsqrt(ms + EPS) * g_ref[...]).astype(BF16)

    acc = jnp.dot(hn_ref[...], w_ref[...], preferred_element_type=F32)

    def seg_norm(y, row):
        sq = y * y
        hi = sq.astype(BF16)
        lo = (sq - hi.astype(F32)).astype(BF16)
        ss = (jnp.dot(hi, bd_ref[...], preferred_element_type=F32)
              + jnp.dot(lo, bd_ref[...], preferred_element_type=F32))
        return y * lax.rsqrt(ss * (1.0 / HEAD_DIM) + EPS) * qkg_ref[pl.ds(row, 1), :]

    def rope(y):
        outs = []
        for c in range(COL_TILE // LANES):
            ys = y[:, c * LANES:(c + 1) * LANES]
            outs.append(ys * rc_ref[...]
                        + pltpu.roll(ys, LANES - ROPE_DIMS // 2, 1) * rs1_ref[...]
                        + pltpu.roll(ys, ROPE_DIMS // 2, 1) * rs2_ref[...])
        return jnp.concatenate(outs, axis=1)

    scale = HEAD_DIM ** -0.5 * LOG2E

    @pl.when(j < n_gate)
    def _():
        o_ref[...] = jax.nn.sigmoid(acc).astype(BF16)

    @pl.when(j == n_gate + 0)
    def _():
        o_ref[...] = (rope(seg_norm(acc, 0)) * scale).astype(BF16)

    @pl.when(j == n_gate + 1)
    def _():
        o_ref[...] = rope(seg_norm(acc, 1)).astype(BF16)

    @pl.when(j == n_gate + 3)
    def _():
        o_ref[...] = (seg_norm(acc, 2) * scale).astype(BF16)

    @pl.when(j == n_gate + 4)
    def _():
        o_ref[...] = seg_norm(acc, 3).astype(BF16)

    @pl.when(j == n_gate + 6)
    def _():
        o_ref[...] = (acc * scale).astype(BF16)

    plain = ((j == n_gate + 2) | (j == n_gate + 5) | (j == n_gate + 7) | (j == n_gate + 8))

    @pl.when(plain)
    def _():
        o_ref[...] = acc.astype(BF16)


def _in_projection(x2d, g1, w, rc, rs1, rs2, qkg, bd, seq, tm):
    T, D = x2d.shape
    ncol = w.shape[1]
    nseq = seq // tm
    ntile = ncol // COL_TILE
    n_gate = GATE_COLS // COL_TILE
    wtile = lambda j: lax.rem(j + ntile - n_gate, ntile)
    return pl.pallas_call(
        _inproj_kernel,
        grid=(T // tm, ntile),
        in_specs=[
            pl.BlockSpec((tm, D), lambda i, j: (i, 0)),
            pl.BlockSpec((1, D), lambda i, j: (0, 0)),
            pl.BlockSpec((D, COL_TILE), lambda i, j: (0, wtile(j))),
            pl.BlockSpec((tm, LANES), lambda i, j: (i % nseq, 0)),
            pl.BlockSpec((tm, LANES), lambda i, j: (i % nseq, 0)),
            pl.BlockSpec((tm, LANES), lambda i, j: (i % nseq, 0)),
            pl.BlockSpec((4, COL_TILE), lambda i, j: (0, 0)),
            pl.BlockSpec((COL_TILE, COL_TILE), lambda i, j: (0, 0)),
        ],
        out_specs=pl.BlockSpec((tm, COL_TILE), lambda i, j: (i, j)),
        out_shape=jax.ShapeDtypeStruct((T, ncol), BF16),
        scratch_shapes=[pltpu.VMEM((tm, D), BF16)],
        compiler_params=_cparams(("parallel", "arbitrary")),
        name="in_projection",
    )(x2d, g1, w, rc, rs1, rs2, qkg, bd)


def _diff_attn_kernel(lam_ref, sg_ref, q_ref, k_ref, v_ref, o_ref, m_ref, l_ref, acc_ref,
                      *, tq, lam_init):
    i = pl.program_id(2)
    q = q_ref[...]
    lane = lax.broadcasted_iota(jnp.int32, (tq, LANES), 1)
    zero = jnp.zeros_like(q)
    qs = jnp.concatenate([jnp.where(lane < HEAD_DIM, q, zero),
                          jnp.where(lane >= HEAD_DIM, q, zero)], axis=0)

    m_ref[...] = jnp.full(m_ref.shape, NEG_BIG, F32)
    l_ref[...] = jnp.zeros(l_ref.shape, F32)
    acc_ref[...] = jnp.zeros(acc_ref.shape, F32)

    nt = tq // LANES

    def step(blocks, last=False):
        ss = []
        for n, j in enumerate(blocks):
            jc = jnp.maximum(j, 0) if last else j
            k = k_ref[pl.ds(pl.multiple_of(jc * tq, tq), tq), :]
            s = lax.dot_general(qs, k, _NT, preferred_element_type=F32)
            if last and n == len(blocks) - 1:
                row = lax.broadcasted_iota(jnp.int32, (2 * tq, tq), 0)
                col = lax.broadcasted_iota(jnp.int32, (2 * tq, tq), 1)
                qrow = jnp.where(row >= tq, row - tq, row)
                shift = CHUNK.bit_length() - 1
                s = jnp.where(jnp.right_shift(col, shift) <= jnp.right_shift(qrow, shift),
                              s, NEG_BIG)
            elif last:
                s = jnp.where(j >= 0, s, NEG_BIG)
            ss.append(s)
        m_prev = m_ref[...]
        smax = functools.reduce(jnp.maximum, [s[:, c * LANES:(c + 1) * LANES]
                                              for s in ss for c in range(nt)])
        m_new = jnp.maximum(m_prev, jnp.max(smax, axis=-1, keepdims=True))
        alpha = jnp.exp2(m_prev - m_new)
        acc = alpha * acc_ref[...]
        psum = None
        for j, s in zip(blocks, ss):
            ps = [jnp.exp2(s[:, c * LANES:(c + 1) * LANES] - m_new) for c in range(nt)]
            psum = functools.reduce(jnp.add, ps if psum is None else [psum] + ps)
            jc = jnp.maximum(j, 0) if last else j
            v = v_ref[pl.ds(pl.multiple_of(jc * tq, tq), tq), :]
            acc = acc + jnp.dot(jnp.concatenate(ps, axis=1).astype(BF16), v,
                                preferred_element_type=F32)
        l_ref[...] = alpha * l_ref[...] + jnp.sum(psum, axis=-1, keepdims=True)
        acc_ref[...] = acc
        m_ref[...] = m_new

    n_body = jnp.maximum(i + 1 - LAST_GROUP, 0)

    def oct_body(t, c):
        step([KV_UNROLL * t + d for d in range(KV_UNROLL)])
        return c

    lax.fori_loop(0, n_body // KV_UNROLL, oct_body, 0)
    rem = n_body % KV_UNROLL
    width = KV_UNROLL // 2
    while width >= 1:
        @pl.when(rem % (2 * width) >= width)
        def _(width=width):
            base = n_body - rem % (2 * width)
            step([base + d for d in range(width)])
        width //= 2

    step([i - (LAST_GROUP - 1) + d for d in range(LAST_GROUP)], last=True)

    lam = (jnp.exp(jnp.sum(lam_ref[0:1, :] * lam_ref[1:2, :], axis=-1, keepdims=True))
           - jnp.exp(jnp.sum(lam_ref[2:3, :] * lam_ref[3:4, :], axis=-1, keepdims=True))
           + lam_init)
    o = acc_ref[...] / l_ref[...]
    o = o[:tq] - lam * o[tq:]
    ms = jnp.mean(o * o, axis=-1, keepdims=True)
    o = o * lax.rsqrt(ms + EPS) * sg_ref[...]
    o_ref[...] = (o * (1.0 - lam_init)).astype(BF16)


def _diff_attention(proj, lamv, subln_g, batch, seq, tq, lam_init):
    T = proj.shape[0]
    nq = seq // tq
    return pl.pallas_call(
        functools.partial(_diff_attn_kernel, tq=tq, lam_init=lam_init),
        grid=(batch, A_HEADS, nq),
        in_specs=[
            pl.BlockSpec((4, HEAD_DIM), lambda b, h, i: (0, 0)),
            pl.BlockSpec((1, LANES), lambda b, h, i: (0, 0)),
            pl.BlockSpec((tq, LANES), lambda b, h, i: (b * nq + i, AQ_BLK + h)),
            pl.BlockSpec((seq, LANES), lambda b, h, i: (b, AK_BLK + h)),
            pl.BlockSpec((seq, LANES), lambda b, h, i: (b, AV_BLK + h)),
        ],
        out_specs=pl.BlockSpec((tq, LANES), lambda b, h, i: (b * nq + i, h)),
        out_shape=jax.ShapeDtypeStruct((T, A_HEADS * LANES), BF16),
        scratch_shapes=[pltpu.VMEM((2 * tq, LANES), F32), pltpu.VMEM((2 * tq, LANES), F32),
                        pltpu.VMEM((2 * tq, LANES), F32)],
        compiler_params=_cparams(("parallel", "parallel", "arbitrary")),
        name="diff_attention",
    )(lamv, subln_g, proj, proj, proj)


def _band_attn_kernel(bm_ref, q_ref, k_ref, v_ref, o_ref, *, tq):
    i = pl.program_id(2)
    q = q_ref[...]
    lane = lax.broadcasted_iota(jnp.int32, (tq, LANES), 1)
    zero = jnp.zeros_like(q)
    nwin = B_LEFT_CHUNKS * CHUNK // tq + 1
    outs = []
    for half in range(2):
        sel = (lane < HEAD_DIM) if half == 0 else (lane >= HEAD_DIM)
        qh = jnp.where(sel, q, zero)
        ss = []
        for blk in range(nwin):
            kb = i - (nwin - 1) + blk
            kbc = jnp.maximum(kb, 0)
            k = k_ref[pl.ds(pl.multiple_of(kbc * tq, tq), tq), :]
            s = lax.dot_general(qh, k, _NT, preferred_element_type=F32)
            s = s + bm_ref[half, :, blk * tq:(blk + 1) * tq]
            ss.append(jnp.where(kb >= 0, s, NEG_BIG))
        s = jnp.concatenate(ss, axis=1)
        m = jnp.max(s, axis=-1, keepdims=True)
        p = jnp.exp2(s - m)
        l = jnp.sum(p, axis=-1, keepdims=True)
        pb = p.astype(BF16)
        o = jnp.zeros((tq, LANES), F32)
        for blk in range(nwin):
            kbc = jnp.maximum(i - (nwin - 1) + blk, 0)
            v = v_ref[pl.ds(pl.multiple_of(kbc * tq, tq), tq), :]
            o = o + jnp.dot(pb[:, blk * tq:(blk + 1) * tq], v, preferred_element_type=F32)
        outs.append(o / l)
    o_ref[...] = jnp.where(lane < HEAD_DIM, outs[0], outs[1]).astype(BF16)


def _band_attention(proj, biasmask, batch, seq, tq):
    T = proj.shape[0]
    nq = seq // tq
    npair = B_HEADS // 2
    wwin = biasmask.shape[-1]
    return pl.pallas_call(
        functools.partial(_band_attn_kernel, tq=tq),
        grid=(batch, npair, nq),
        in_specs=[
            pl.BlockSpec((2, tq, wwin), lambda b, h, i: (h, 0, 0)),
            pl.BlockSpec((tq, LANES), lambda b, h, i: (b * nq + i, BQ_BLK + h)),
            pl.BlockSpec((seq, LANES), lambda b, h, i: (b, BK_BLK + h)),
            pl.BlockSpec((seq, LANES), lambda b, h, i: (b, BV_BLK + h)),
        ],
        out_specs=pl.BlockSpec((tq, LANES), lambda b, h, i: (b * nq + i, h)),
        out_shape=jax.ShapeDtypeStruct((T, npair * LANES), BF16),
        compiler_params=_cparams(("parallel", "parallel", "arbitrary")),
        name="band_attention",
    )(biasmask, proj, proj, proj)


def _stick_attn_kernel(tri_ref, q_ref, k_ref, v_ref, o_ref, carry_ref, acc_ref, *, tq):
    i = pl.program_id(2)
    q = q_ref[...]
    lane = lax.broadcasted_iota(jnp.int32, (tq, LANES), 1)
    zero = jnp.zeros_like(q)
    qh = (jnp.where(lane < HEAD_DIM, q, zero), jnp.where(lane >= HEAD_DIM, q, zero))

    carry_ref[...] = jnp.zeros(carry_ref.shape, F32)
    acc_ref[...] = jnp.zeros(acc_ref.shape, F32)

    nt = tq // LANES
    lane_tiles = lambda t: [t[:, c * LANES:(c + 1) * LANES] for c in range(nt)]

    def step(blocks, first=False):
        if first:
            row = lax.broadcasted_iota(jnp.int32, (tq, tq), 0)
            col = lax.broadcasted_iota(jnp.int32, (tq, tq), 1)
            strict = col < row
        for half in range(2):
            parts = []
            for n, j in enumerate(blocks):
                keep = None
                if first:
                    keep = strict if n == 0 else (j >= 0)
                jc = jnp.maximum(j, 0) if first else j
                k = k_ref[pl.ds(pl.multiple_of(jc * tq, tq), tq), :]
                z2 = lax.dot_general(qh[half], k, _NT, preferred_element_type=F32)
                nz2 = -z2
                lg2 = jnp.log(1.0 + jnp.exp2(jnp.minimum(z2, nz2))) * LOG2E
                lk = jnp.minimum(nz2, 0.0) - lg2
                ls = lk + z2
                if keep is not None:
                    lk = jnp.where(keep, lk, 0.0)
                later = jnp.dot(lk.astype(BF16), tri_ref[...], preferred_element_type=F32)
                tot = jnp.sum(functools.reduce(jnp.add, lane_tiles(lk)),
                              axis=-1, keepdims=True)
                parts.append((jc, keep, ls, later, tot))
            carry = carry_ref[half]
            acc = acc_ref[half]
            for jc, keep, ls, later, tot in parts:
                a = jnp.concatenate([jnp.exp2(x + (y + carry))
                                     for x, y in zip(lane_tiles(ls), lane_tiles(later))], axis=1)
                if keep is not None:
                    a = jnp.where(keep, a, 0.0)
                v = v_ref[pl.ds(pl.multiple_of(jc * tq, tq), tq), :]
                acc = acc + jnp.dot(a.astype(BF16), v, preferred_element_type=F32)
                carry = carry + tot
            acc_ref[half] = acc
            carry_ref[half] = carry

    step([i - d for d in range(LAST_GROUP)], first=True)

    n_body = jnp.maximum(i + 1 - LAST_GROUP, 0)

    def oct_body(t, c):
        j = n_body - 1 - KV_UNROLL * t
        step([j - d for d in range(KV_UNROLL)])
        return c

    lax.fori_loop(0, n_body // KV_UNROLL, oct_body, 0)
    rem = n_body % KV_UNROLL
    width = KV_UNROLL // 2
    while width >= 1:
        @pl.when(rem % (2 * width) >= width)
        def _(width=width):
            top = rem % (2 * width) - 1
            step([top - d for d in range(width)])
        width //= 2
    o_ref[...] = jnp.where(lane < HEAD_DIM, acc_ref[0], acc_ref[1]).astype(BF16)


def _stick_attention(proj, tri, batch, seq, tq):
    T = proj.shape[0]
    nq = seq // tq
    npair = C_HEADS // 2
    return pl.pallas_call(
        functools.partial(_stick_attn_kernel, tq=tq),
        grid=(batch, npair, nq),
        in_specs=[
            pl.BlockSpec((tq, tq), lambda b, h, i: (0, 0)),
            pl.BlockSpec((tq, LANES), lambda b, h, i: (b * nq + i, CQ_BLK + h)),
            pl.BlockSpec((seq, LANES), lambda b, h, i: (b, CK_BLK + h)),
            pl.BlockSpec((seq, LANES), lambda b, h, i: (b, CV_BLK + h)),
        ],
        out_specs=pl.BlockSpec((tq, LANES), lambda b, h, i: (b * nq + i, h)),
        out_shape=jax.ShapeDtypeStruct((T, npair * LANES), BF16),
        scratch_shapes=[pltpu.VMEM((2, tq, LANES), F32), pltpu.VMEM((2, tq, LANES), F32)],
        compiler_params=_cparams(("parallel", "parallel", "arbitrary")),
        name="stick_attention",
    )(tri, proj, proj, proj)


def _merge_kernel(x_ref, ya_ref, yb_ref, yc_ref, g0_ref, g1_ref, g2_ref, wb_ref, wo_ref,
                  n2_ref, xo_ref, h2_ref):
    merged = None
    for n, (y_ref, g_ref) in enumerate(((ya_ref, g0_ref), (yb_ref, g1_ref), (yc_ref, g2_ref))):
        up = jnp.dot(y_ref[...], wb_ref[n], preferred_element_type=F32)
        term = g_ref[...].astype(F32) * up
        merged = term if merged is None else merged + term
    xn = x_ref[...] + jnp.dot(merged.astype(BF16), wo_ref[...], preferred_element_type=F32)
    xo_ref[...] = xn
    ms = jnp.mean(xn * xn, axis=-1, keepdims=True)
    h2_ref[...] = (xn * lax.rsqrt(ms + EPS) * n2_ref[...]).astype(BF16)


def _merge(x2d, ya, yb, yc, proj, wb, wo, n2, tm):
    T, D = x2d.shape
    W = ya.shape[1]
    yspec = pl.BlockSpec((tm, W), lambda i: (i, 0))
    return pl.pallas_call(
        _merge_kernel,
        grid=(T // tm,),
        in_specs=[
            pl.BlockSpec((tm, D), lambda i: (i, 0)),
            yspec, yspec, yspec,
            pl.BlockSpec((tm, D), lambda i: (i, 0)),
            pl.BlockSpec((tm, D), lambda i: (i, 1)),
            pl.BlockSpec((tm, D), lambda i: (i, 2)),
            pl.BlockSpec((N_BRANCH, W, D), lambda i: (0, 0, 0)),
            pl.BlockSpec((D, D), lambda i: (0, 0)),
            pl.BlockSpec((1, D), lambda i: (0, 0)),
        ],
        out_specs=[pl.BlockSpec((tm, D), lambda i: (i, 0)),
                   pl.BlockSpec((tm, D), lambda i: (i, 0))],
        out_shape=[jax.ShapeDtypeStruct((T, D), F32), jax.ShapeDtypeStruct((T, D), BF16)],
        compiler_params=_cparams(("parallel",)),
        name="gated_merge",
    )(x2d, ya, yb, yc, proj, proj, proj, wb, wo, n2)


def _candidate_pairs():
    return [(a, b) for a in range(PEER_TOPK) for b in range(PEER_TOPK)
            if (a + 1) * (b + 1) <= PEER_TOPK]


def _peer_route_kernel(h2_ref, wqt_ref, sk_ref, c1_ref, r2_ref, e1_ref, e2_ref,
                       s_ref, xw_ref, rk_ref, top_ref):
    qT = lax.dot_general(wqt_ref[...], h2_ref[...], _NT,
                         preferred_element_type=F32).astype(BF16)
    kio = lax.broadcasted_iota(jnp.int32, (N_KEYS, LANES), 0).astype(F32)
    pairs = _candidate_pairs()

    group = 4
    for h0 in range(0, PEER_HEADS, group):
        probs = [(h, p) for h in range(h0, h0 + group) for p in range(2)]
        for h, p in probs:
            r0 = (h * 2 + p) * PEER_HALF
            x0 = jnp.dot(sk_ref[p, h], qT[r0:r0 + PEER_HALF, :],
                         preferred_element_type=F32)
            s_ref[p, h] = x0
            xw_ref[p, h] = x0
            rk_ref[p, h] = jnp.full((N_KEYS, LANES), float(PEER_TOPK), F32)

        def it(r, c, probs=probs):
            rf = jnp.asarray(r, F32)
            for h, p in probs:
                x = xw_ref[p, h]
                m = jnp.max(x, axis=0, keepdims=True)
                idx = jnp.where(x == m, kio, float(N_KEYS))
                first = jnp.min(idx, axis=0, keepdims=True)
                hit = kio == first
                top_ref[p, r, pl.ds(h, 1), :] = m
                xw_ref[p, h] = jnp.where(hit, -jnp.inf, x)
                rk_ref[p, h] = jnp.where(hit, rf, rk_ref[p, h])
            return c

        lax.fori_loop(0, PEER_TOPK, it, 0)
    ranks = {h: rk_ref[0, h] for h in range(PEER_HEADS)}
    for h in range(PEER_HEADS):
        r2_ref[h] = rk_ref[1, h].astype(BF16)

    a = [top_ref[0, r] for r in range(PEER_TOPK)]
    b = [top_ref[1, r] for r in range(PEER_TOPK)]
    cand = [a[ia] + b[ib] for (ia, ib) in pairs]
    m0 = cand[0]
    cnt = [jnp.zeros_like(m0) for _ in range(PEER_TOPK)]
    z = jnp.zeros_like(m0)
    for _ in range(PEER_TOPK):
        m = functools.reduce(jnp.maximum, cand)
        z = z + jnp.exp(m - m0)
        found = jnp.zeros(m0.shape, jnp.bool_)
        for ci, (ia, ib) in enumerate(pairs):
            hit = (cand[ci] == m) & jnp.logical_not(found)
            found = found | hit
            cand[ci] = jnp.where(hit, -jnp.inf, cand[ci])
            cnt[ia] = cnt[ia] + jnp.where(hit, 1.0, 0.0)
    inv_z = 0.5 / z

    for h in range(PEER_HEADS):
        c1 = jnp.zeros((N_KEYS, LANES), F32)
        for r in range(PEER_TOPK):
            c1 = jnp.where(ranks[h] == float(r), cnt[r][h:h + 1, :], c1)
        c1_ref[h] = c1
        e1_ref[h] = jnp.exp(s_ref[0, h] - a[0][h:h + 1, :])
        e2_ref[h] = (jnp.exp(s_ref[1, h] - b[0][h:h + 1, :]) * inv_z[h:h + 1, :]).astype(BF16)


def _peer_route(h2, wqt, sk):
    T, D = h2.shape
    oshape_w = jax.ShapeDtypeStruct((PEER_HEADS, N_KEYS, T), F32)
    oshape_bf = jax.ShapeDtypeStruct((PEER_HEADS, N_KEYS, T), BF16)
    ospec = pl.BlockSpec((PEER_HEADS, N_KEYS, LANES), lambda i: (0, 0, i))
    return pl.pallas_call(
        _peer_route_kernel,
        grid=(T // LANES,),
        in_specs=[
            pl.BlockSpec((LANES, D), lambda i: (i, 0)),
            pl.BlockSpec(wqt.shape, lambda i: (0, 0)),
            pl.BlockSpec(sk.shape, lambda i: (0, 0, 0, 0)),
        ],
        out_specs=[ospec, ospec, ospec, ospec],
        out_shape=[oshape_w, oshape_bf, oshape_w, oshape_bf],
        scratch_shapes=[pltpu.VMEM((2, PEER_HEADS, N_KEYS, LANES), F32),
                        pltpu.VMEM((2, PEER_HEADS, N_KEYS, LANES), F32),
                        pltpu.VMEM((2, PEER_HEADS, N_KEYS, LANES), F32),
                        pltpu.VMEM((2, PEER_TOPK, PEER_HEADS, LANES), F32)],
        compiler_params=_cparams(("parallel",)),
        name="peer_route",
    )(h2, wqt, sk)


def _peer_dense_kernel(x_ref, h2_ref, u_ref, vt_ref, c1_ref, r2_ref, e1_ref, e2_ref,
                       o_ref, acc_ref, g_ref, w_ref, h_ref, cw_ref, ew_ref, *, tm, ce, nchunk):
    c = pl.program_id(0)
    pack = 16
    span = 4 * pack
    nt = tm // LANES
    n_i1 = ce // N_KEYS

    @pl.when(c == 0)
    def _():
        acc_ref[...] = jnp.zeros(acc_ref.shape, F32)
        g_ref[1] = jnp.zeros(g_ref.shape[1:], BF16)

    def bcast_bf16(ref, h, row):
        tiles = []
        for t in range(nt):
            x = ref[pl.ds((h * nt + t) * n_i1 + row, pack // 2, stride=0), :]
            tiles.append(jnp.concatenate([x, x], axis=0).astype(BF16))
        return jnp.concatenate(tiles, axis=1)

    cur = lax.rem(c, 2)

    for h in range(PEER_HEADS):
        for t in range(nt):
            slab = pl.ds((h * nt + t) * n_i1, n_i1)
            cw_ref[slab, :] = c1_ref[h, :, t * LANES:(t + 1) * LANES]
            ew_ref[slab, :] = e1_ref[h, :, t * LANES:(t + 1) * LANES]

    zero = jnp.zeros((pack, tm), BF16)

    def w_build(ii):
        for r0 in range(0, N_KEYS, span):
            w = [zero for _ in range(span // pack)]
            for h in range(PEER_HEADS):
                c1b = bcast_bf16(cw_ref, h, ii)
                e1b = bcast_bf16(ew_ref, h, ii)
                for g in range(span // pack):
                    rows = slice(r0 + g * pack, r0 + (g + 1) * pack)
                    w[g] = w[g] + e1b * jnp.where(r2_ref[h, rows, :] < c1b,
                                                  e2_ref[h, rows, :], zero)
            e0 = ii * N_KEYS + r0
            w_ref[e0:e0 + span, :] = jnp.concatenate(w, axis=0)

    half = tm // 2
    pieces = []
    for s in range(2):
        tok = slice(s * half, (s + 1) * half)
        pieces.append(("drain", tok))
    for s in range(2):
        tok = slice(s * half, (s + 1) * half)
        pieces.append(("score", tok))
    per = n_i1 // len(pieces)
    for n, (kind, tok) in enumerate(pieces):
        @pl.when(c >= 0)
        def _(n=n, kind=kind, tok=tok):
            if kind == "drain":
                acc_ref[:, tok] += jnp.dot(vt_ref[...], g_ref[1 - cur, :, tok],
                                           preferred_element_type=F32)
            else:
                h_ref[:, tok] = lax.dot_general(u_ref[...], h2_ref[tok, :], _NT,
                                                preferred_element_type=F32)
            for ii in range(n * per, (n + 1) * per):
                w_build(ii)

    def g_body(k, carry):
        rows = pl.ds(pl.multiple_of(k * span, span), span)
        hh = h_ref[rows, :]
        act = (hh + hh * lax.erf(hh * (1.0 / math.sqrt(2.0)))).astype(BF16)
        g_ref[cur, rows, :] = w_ref[rows, :] * act
        return carry

    lax.fori_loop(0, ce // span, g_body, 0)

    @pl.when((c > 0) & (lax.rem(c, nchunk) == 0))
    def _():
        o_ref[...] = x_ref[...] + acc_ref[...].T
        acc_ref[...] = jnp.zeros(acc_ref.shape, F32)


def _peer_dense(x2d, h2, u, vt, c1, r2, e1, e2, tm, ce):
    T, D = x2d.shape
    ne = u.shape[0]
    nchunk = ne // ce
    nitem = (T // tm) * nchunk
    build = lambda c: jnp.minimum(c, nitem - 1)
    drain = lambda c: jnp.maximum(c - 1, 0)
    tile = lambda item: item // nchunk
    chunk = lambda item: lax.rem(item, nchunk)
    rspec = pl.BlockSpec((PEER_HEADS, N_KEYS, tm), lambda c: (0, 0, tile(build(c))))
    cspec = pl.BlockSpec((PEER_HEADS, ce // N_KEYS, tm),
                         lambda c: (0, chunk(build(c)), tile(build(c))))
    return pl.pallas_call(
        functools.partial(_peer_dense_kernel, tm=tm, ce=ce, nchunk=nchunk),
        grid=(nitem + 1,),
        in_specs=[
            pl.BlockSpec((tm, D), lambda c: (tile(drain(c)), 0)),
            pl.BlockSpec((tm, D), lambda c: (tile(build(c)), 0)),
            pl.BlockSpec((ce, D), lambda c: (chunk(build(c)), 0)),
            pl.BlockSpec((None, D, ce), lambda c: (chunk(drain(c)), 0, 0)),
            cspec, rspec, cspec, rspec,
        ],
        out_specs=pl.BlockSpec((tm, D), lambda c: (tile(drain(c)), 0)),
        out_shape=jax.ShapeDtypeStruct((T, D), F32),
        scratch_shapes=[pltpu.VMEM((D, tm), F32),
                        pltpu.VMEM((2, ce, tm), BF16),
                        pltpu.VMEM((ce, tm), BF16),
                        pltpu.VMEM((ce, tm), F32),
                        pltpu.VMEM((PEER_HEADS * (tm // LANES) * (ce // N_KEYS), LANES), F32),
                        pltpu.VMEM((PEER_HEADS * (tm // LANES) * (ce // N_KEYS), LANES), F32)],
        compiler_params=_cparams(("arbitrary",)),
        name="peer_dense",
    )(x2d, h2, u, vt, c1, r2, e1, e2)


def _rope_tables(seq):
    inv = 1.0 / (ROPE_THETA ** (jnp.arange(0, ROPE_DIMS, 2, dtype=F32) / ROPE_DIMS))
    ang = jnp.arange(seq, dtype=F32)[:, None] * inv[None, :]
    cos, sin = jnp.cos(ang), jnp.sin(ang)
    half = ROPE_DIMS // 2
    pad = HEAD_DIM - ROPE_DIMS
    one = jnp.ones((seq, pad), F32)
    zer = jnp.zeros((seq, pad), F32)
    zh = jnp.zeros((seq, half), F32)
    rc = jnp.concatenate([cos, cos, one], axis=1)
    rs1 = jnp.concatenate([-sin, zh, zer], axis=1)
    rs2 = jnp.concatenate([zh, sin, zer], axis=1)
    rep = LANES // HEAD_DIM
    return jnp.tile(rc, (1, rep)), jnp.tile(rs1, (1, rep)), jnp.tile(rs2, (1, rep))


def _band_bias_mask(rel_bias, tq):
    nwin = B_LEFT_CHUNKS * CHUNK // tq + 1
    r = np.arange(tq)[:, None]
    c = np.arange(nwin * tq)[None, :]
    kchunk = c // CHUNK - (nwin - 1) * (tq // CHUNK)
    qchunk = r // CHUNK
    valid = (kchunk <= qchunk) & (kchunk >= qchunk - B_LEFT_CHUNKS)
    wwin, d0 = nwin * tq, (nwin - 1) * tq
    period = wwin + tq
    u = np.arange(-(tq - 1), wwin)
    perm = np.zeros(period, np.int32)
    perm[u % period] = np.clip(d0 - u, -REL_CLIP, REL_CLIP) + REL_CLIP
    line = rel_bias.astype(F32)[:, perm]
    nh = rel_bias.shape[0]
    skew = jnp.broadcast_to(line[:, None, :], (nh, tq, period)).reshape(nh, tq * period)
    bias = skew[:, :tq * (period - 1)].reshape(nh, tq, period - 1)[:, :, :wwin]
    return jnp.where(jnp.asarray(valid)[None], bias * LOG2E, NEG_BIG)


def _layer(x2d, p, layer, batch, seq, consts):
    T, D = x2d.shape
    lam_init = 0.8 - 0.6 * math.exp(-0.3 * layer)
    tq = consts["tq"]

    w_perm = p["w_in"].astype(BF16)
    tile_g = lambda g: jnp.tile(g.astype(F32), COL_TILE // HEAD_DIM)
    qkg = jnp.stack([tile_g(p["a_qnorm_g"]), tile_g(p["a_knorm_g"]),
                     tile_g(p["b_qnorm_g"]), tile_g(p["b_knorm_g"])])
    proj = _in_projection(x2d, p["norm1_g"].reshape(1, D).astype(F32), w_perm,
                          consts["rc"], consts["rs1"], consts["rs2"], qkg, consts["bd"],
                          seq, consts["tm_in"])

    lamv = jnp.stack([p["lam_q1"], p["lam_k1"], p["lam_q2"], p["lam_k2"]]).astype(F32)
    ya = _diff_attention(proj, lamv, p["a_subln_g"].reshape(1, LANES).astype(F32),
                         batch, seq, tq, lam_init)
    yb = _band_attention(proj, _band_bias_mask(p["b_rel_bias"], tq), batch, seq, tq)
    yc = _stick_attention(proj, consts["tri"], batch, seq, tq)

    xn, h2 = _merge(x2d, ya, yb, yc, proj, p["w_branch"].astype(BF16),
                    p["w_out"].astype(BF16), p["norm2_g"].reshape(1, D).astype(F32),
                    consts["tm_merge"])

    wqt = p["peer_wq"].T.astype(BF16)
    sk = p["peer_subkeys"].astype(BF16)
    c1, r2, e1, e2 = _peer_route(h2, wqt, sk)
    u = p["peer_u"].astype(BF16)
    ce = consts["ce"]
    vt = jnp.transpose(p["peer_v"].astype(BF16).reshape(-1, ce, D), (0, 2, 1))
    return _peer_dense(xn, h2, u, vt, c1, r2, e1, e2, consts["tm_peer"], consts["ce"])


def kernel(x, norm1_g, w_in, a_qnorm_g, a_knorm_g, lam_q1, lam_k1, lam_q2, lam_k2, a_subln_g, b_qnorm_g, b_knorm_g, b_rel_bias, w_branch, w_out, norm2_g, peer_wq, peer_subkeys, peer_u, peer_v):
    B, S, D = x.shape
    T = B * S
    depth = w_in.shape[0]
    tq = min(256, S)
    rc, rs1, rs2 = _rope_tables(S)
    seg = np.arange(COL_TILE) // HEAD_DIM
    bd = jnp.asarray(seg[:, None] == seg[None, :], BF16)
    kk = np.arange(tq)
    tri = jnp.asarray(kk[:, None] > kk[None, :], BF16)
    consts = dict(tq=tq, rc=rc, rs1=rs1, rs2=rs2, bd=bd, tri=tri,
                  tm_in=min(1024, S), tm_merge=min(512, T), tm_peer=min(512, T), ce=1024)
    params = dict(norm1_g=norm1_g, w_in=w_in, a_qnorm_g=a_qnorm_g, a_knorm_g=a_knorm_g,
                  lam_q1=lam_q1, lam_k1=lam_k1, lam_q2=lam_q2, lam_k2=lam_k2,
                  a_subln_g=a_subln_g, b_qnorm_g=b_qnorm_g, b_knorm_g=b_knorm_g,
                  b_rel_bias=b_rel_bias, w_branch=w_branch, w_out=w_out, norm2_g=norm2_g,
                  peer_wq=peer_wq, peer_subkeys=peer_subkeys, peer_u=peer_u, peer_v=peer_v)
    x2d = x.reshape(T, D)
    for layer in range(depth):
        x2d = _layer(x2d, {k: v[layer] for k, v in params.items()}, layer, B, S, consts)
    return x2d.reshape(B, S, D)
```
